```python
import jax, jax.numpy as jnp
from jax import lax
import numpy as np


D_MODEL = 2048
BATCH = 8
SEQ = 2048
DEPTH = 2
DEC_BATCH = 32
DEC_SEQ = 16
PAST_LEN = 4096

CHUNK = 64
N_EVEN = (DEPTH + 1) // 2
N_ODD = DEPTH // 2
D_CONV = D_MODEL // 2
D_POOL = D_MODEL // 2
CONV_WIDTH = 31
POOL_WINDOWS = (2, 4, 8, 16)
N_POOL_GROUPS = 4
POOL_GROUP = D_POOL // N_POOL_GROUPS
POOL_MAX = 16
HGRN_DK = 128
HGRN_DV = 128
HGRN_HEADS = D_MODEL // HGRN_DK
HGRN_DIM = HGRN_HEADS * HGRN_DK
D_FF = 5632
N_SUBNORMS = 6
EPS = 1e-6

kernel_name = "hybrid_streaming_conv_pool_hgrn2_step"


def rms_norm(x, g):
    xf = x.astype(jnp.float32)
    y = xf * lax.rsqrt(jnp.mean(xf * xf, axis=-1, keepdims=True) + EPS)
    return (y * g.astype(jnp.float32)).astype(x.dtype)


def layer_norm(x, g, b):
    xf = x.astype(jnp.float32)
    mu = jnp.mean(xf, axis=-1, keepdims=True)
    xc = xf - mu
    y = xc * lax.rsqrt(jnp.mean(xc * xc, axis=-1, keepdims=True) + EPS)
    return (y * g.astype(jnp.float32) + b.astype(jnp.float32)).astype(x.dtype)


def swiglu_ffn(x, w_in, w_out):
    a, b = jnp.split(x @ w_in, 2, axis=-1)
    return (jax.nn.silu(a) * b) @ w_out


def conformer_conv(a, hist, w, bias, ln_g, ln_b):
    seq = jnp.concatenate([hist.astype(a.dtype), a], axis=1)
    y = lax.conv_general_dilated(seq, w[:, None, :].astype(a.dtype), window_strides=(1,), padding='VALID',
                                 dimension_numbers=('NWC', 'WIO', 'NWC'), feature_group_count=D_CONV)
    y = jax.nn.silu(layer_norm(y + bias.astype(a.dtype), ln_g, ln_b))
    return y, seq[:, -(CONV_WIDTH - 1):]


def multiscale_pool(p, hist, start_pos, w, scale):
    B, L, _ = p.shape
    seq = jnp.concatenate([hist.astype(p.dtype), p], axis=1)
    cs = jnp.concatenate([jnp.zeros((B, 1, D_POOL), jnp.float32),
                          jnp.cumsum(seq.astype(jnp.float32), axis=1)], axis=1)
    hi = cs[:, POOL_MAX:]
    pos = start_pos + jnp.arange(L, dtype=jnp.int32)
    means = []
    for gi, win in enumerate(POOL_WINDOWS):
        sl = slice(gi * POOL_GROUP, (gi + 1) * POOL_GROUP)
        wsum = hi[..., sl] - cs[:, POOL_MAX - win:POOL_MAX - win + L, sl]
        cnt = jnp.minimum(win, pos + 1).astype(jnp.float32)[None, :, None]
        means.append(wsum / cnt)
    y = jnp.concatenate(means, axis=-1) - p.astype(jnp.float32)
    y = jnp.einsum('blgc,gcd->blgd', y.reshape(B, L, N_POOL_GROUPS, POOL_GROUP), w.astype(jnp.float32))
    y = y.reshape(B, L, D_POOL) * scale.astype(jnp.float32)
    return y.astype(p.dtype), seq[:, -(POOL_MAX - 1):]


def _intra_step(s, inp):
    q_t, k_t, v_t, lf_t = inp
    s = jnp.exp(lf_t)[..., None] * s + k_t[..., :, None] * v_t[..., None, :]
    return s, jnp.einsum('bnhk,bnhkv->bnhv', q_t, s)


def _chunk_step(S, inp):
    decay, local = inp
    return decay[..., None] * S + local, S


def hgrn2(xn, S0, w_in, w_out, g_norm, lb):
    B, L, _ = xn.shape
    f32 = jnp.float32
    q, fz, v, gz = jnp.split(xn @ w_in, 4, axis=-1)
    lb = lb.astype(f32).reshape(HGRN_HEADS, HGRN_DK)
    f = lb + (1.0 - lb) * jax.nn.sigmoid(fz.astype(f32).reshape(B, L, HGRN_HEADS, HGRN_DK))
    k = 1.0 - f
    logf = jnp.log(f)
    q = jax.nn.silu(q.astype(f32)).reshape(B, L, HGRN_HEADS, HGRN_DK) * (HGRN_DK ** -0.5)
    v = v.astype(f32).reshape(B, L, HGRN_HEADS, HGRN_DV)
    nc = -(-L // CHUNK)
    pad = nc * CHUNK - L

    def to_chunks(t):
        t = jnp.pad(t, ((0, 0), (0, pad), (0, 0), (0, 0)))
        return t.reshape(B, nc, CHUNK, HGRN_HEADS, t.shape[-1])

    qc, kc, vc, lfc = to_chunks(q), to_chunks(k), to_chunks(v), to_chunks(logf)
    s_zero = jnp.zeros((B, nc, HGRN_HEADS, HGRN_DK, HGRN_DV), f32)
    s_local, o_local = lax.scan(_intra_step, s_zero,
                                (jnp.moveaxis(qc, 2, 0), jnp.moveaxis(kc, 2, 0),
                                 jnp.moveaxis(vc, 2, 0), jnp.moveaxis(lfc, 2, 0)))
    o_local = jnp.moveaxis(o_local, 0, 2)
    b = jnp.cumsum(lfc, axis=2)
    decay = jnp.exp(b[:, :, -1])
    S_final, S_start = lax.scan(_chunk_step, S0.astype(f32),
                                (jnp.moveaxis(decay, 1, 0), jnp.moveaxis(s_local, 1, 0)))
    o_inter = jnp.einsum('bnchk,nbhkv->bnchv', qc * jnp.exp(b), S_start)
    o = (o_local + o_inter).reshape(B, nc * CHUNK, HGRN_HEADS, HGRN_DV)[:, :L]
    o = o * lax.rsqrt(jnp.mean(o * o, axis=-1, keepdims=True) + EPS) * g_norm.astype(f32)
    o = o.reshape(B, L, HGRN_DIM) * jax.nn.silu(gz.astype(f32))
    return o.astype(xn.dtype) @ w_out, S_final.astype(S0.dtype)


def trunk(x, conv_hist, pool_hist, hgrn_S, start_pos, ab_w_in, ab_w_out, conv_w, conv_b, conv_ln_g,
          conv_ln_b, pool_w, pool_scale, hgrn_w_in, hgrn_w_out, hgrn_gnorm, hgrn_lb, ffn_w_in, ffn_w_out,
          norm_g):
    lb_all = jnp.cumsum(jax.nn.softmax(hgrn_lb.astype(jnp.float32), axis=0), axis=0)
    lb_all = lb_all - lb_all[0:1]
    conv_new, pool_new, hgrn_new = [], [], []
    for l in range(DEPTH):
        g = norm_g[l]
        h = swiglu_ffn(rms_norm(x, g[0]), ffn_w_in[l, 0], ffn_w_out[l, 0])
        x = x + 0.5 * rms_norm(h, g[1])
        h = rms_norm(x, g[2])
        if l % 2 == 0:
            e = l // 2
            u = h @ ab_w_in[e]
            a_val, a_gate, p = jnp.split(u, [D_CONV, 2 * D_CONV], axis=-1)
            a = a_val * jax.nn.sigmoid(a_gate)
            a_out, c_new = conformer_conv(a, conv_hist[e], conv_w[e], conv_b[e], conv_ln_g[e], conv_ln_b[e])
            p_out, p_new = multiscale_pool(p, pool_hist[e], start_pos, pool_w[e], pool_scale[e])
            m = jnp.concatenate([a_out, p_out], axis=-1) @ ab_w_out[e]
            conv_new.append(c_new)
            pool_new.append(p_new)
        else:
            o = l // 2
            m, s_new = hgrn2(h, hgrn_S[o], hgrn_w_in[o], hgrn_w_out[o], hgrn_gnorm[o], lb_all[l])
            hgrn_new.append(s_new)
        x = x + rms_norm(m, g[3])
        h = swiglu_ffn(rms_norm(x, g[4]), ffn_w_in[l, 1], ffn_w_out[l, 1])
        x = x + 0.5 * rms_norm(h, g[5])
    return x, jnp.stack(conv_new), jnp.stack(pool_new), jnp.stack(hgrn_new)


def setup_inputs(seed: int = 0) -> dict:
    key = jax.random.key(seed)
    ks = jax.random.split(key, 20)

    def nrm(k, shape, s):
        return jax.random.normal(k, shape, jnp.float32) * s

    return {
        'x_prompt': nrm(ks[0], (BATCH, SEQ, D_MODEL), 1.0),
        'x_sample': nrm(ks[1], (DEC_BATCH, DEC_SEQ, D_MODEL), 1.0),
        'cache_conv': nrm(ks[2], (N_EVEN, DEC_BATCH, CONV_WIDTH - 1, D_CONV), 0.5),
        'cache_pool': nrm(ks[3], (N_EVEN, DEC_BATCH, POOL_MAX - 1, D_POOL), 1.0),
        'state_hgrn': nrm(ks[4], (N_ODD, DEC_BATCH, HGRN_HEADS, HGRN_DK, HGRN_DV), 0.5),
        'ab_w_in': nrm(ks[5], (N_EVEN, D_MODEL, 2 * D_CONV + D_POOL), D_MODEL ** -0.5),
        'ab_w_out': nrm(ks[6], (N_EVEN, D_CONV + D_POOL, D_MODEL), (D_CONV + D_POOL) ** -0.5),
        'conv_w': nrm(ks[7], (N_EVEN, CONV_WIDTH, D_CONV), CONV_WIDTH ** -0.5),
        'conv_b': nrm(ks[8], (N_EVEN, D_CONV), 0.02),
        'conv_ln_g': 1.0 + nrm(ks[9], (N_EVEN, D_CONV), 0.02),
        'conv_ln_b': nrm(ks[10], (N_EVEN, D_CONV), 0.02),
        'pool_w': nrm(ks[11], (N_EVEN, N_POOL_GROUPS, POOL_GROUP, POOL_GROUP), POOL_GROUP ** -0.5),
        'pool_scale': 1.0 + nrm(ks[12], (N_EVEN, D_POOL), 0.1),
        'hgrn_w_in': nrm(ks[13], (N_ODD, D_MODEL, 4 * HGRN_DIM), D_MODEL ** -0.5),
        'hgrn_w_out': nrm(ks[14], (N_ODD, HGRN_DIM, D_MODEL), HGRN_DIM ** -0.5),
        'hgrn_gnorm': 1.0 + nrm(ks[15], (N_ODD, HGRN_DV), 0.02),
        'hgrn_lb': nrm(ks[16], (DEPTH, HGRN_DIM), 0.1),
        'ffn_w_in': nrm(ks[17], (DEPTH, 2, D_MODEL, 2 * D_FF), D_MODEL ** -0.5),
        'ffn_w_out': nrm(ks[18], (DEPTH, 2, D_FF, D_MODEL), D_FF ** -0.5),
        'norm_g': 1.0 + nrm(ks[19], (DEPTH, N_SUBNORMS, D_MODEL), 0.02),
    }


def reference(x_prompt, x_sample, cache_conv, cache_pool, state_hgrn, ab_w_in, ab_w_out, conv_w, conv_b,
              conv_ln_g, conv_ln_b, pool_w, pool_scale, hgrn_w_in, hgrn_w_out, hgrn_gnorm, hgrn_lb, ffn_w_in,
              ffn_w_out, norm_g):
    B = x_prompt.shape[0]
    dt = x_prompt.dtype
    zero_conv = jnp.zeros((N_EVEN, B, CONV_WIDTH - 1, D_CONV), dt)
    zero_pool = jnp.zeros((N_EVEN, B, POOL_MAX - 1, D_POOL), dt)
    zero_hgrn = jnp.zeros((N_ODD, B, HGRN_HEADS, HGRN_DK, HGRN_DV), state_hgrn.dtype)
    y_prompt, conv_p, pool_p, hgrn_p = trunk(
        x_prompt, zero_conv, zero_pool, zero_hgrn, 0, ab_w_in, ab_w_out, conv_w, conv_b, conv_ln_g,
        conv_ln_b, pool_w, pool_scale, hgrn_w_in, hgrn_w_out, hgrn_gnorm, hgrn_lb, ffn_w_in, ffn_w_out, norm_g)
    y_sample, conv_s, pool_s, hgrn_s = trunk(
        x_sample, cache_conv, cache_pool, state_hgrn, PAST_LEN, ab_w_in, ab_w_out, conv_w, conv_b, conv_ln_g,
        conv_ln_b, pool_w, pool_scale, hgrn_w_in, hgrn_w_out, hgrn_gnorm, hgrn_lb, ffn_w_in, ffn_w_out, norm_g)
    return (y_prompt, y_sample, conv_p, pool_p, hgrn_p, conv_s, pool_s, hgrn_s)
```

```python
import functools

import jax
import jax.numpy as jnp
from jax import lax
from jax.experimental import pallas as pl
from jax.experimental.pallas import tpu as pltpu

F32 = jnp.float32
BF16 = jnp.bfloat16

D_MODEL = 2048
D_CONV = 1024
D_POOL = 1024
CONV_WIDTH = 31
POOL_WINDOWS = (2, 4, 8, 16)
POOL_GROUP = 256
POOL_MAX = 16
HEADS = 16
DK = 128
DV = 128
D_FF = 5632
EPS = 1e-6
PAST_LEN = 4096

VMEM_LIMIT_BYTES = 52 * 1024 * 1024
ROW_TILE = 512
CONV_HALO = 32
POOL_HALO = 16
SUBLANES = 8


def _rms(x, g):
    return x * lax.rsqrt(jnp.mean(x * x, axis=-1, keepdims=True) + EPS) * g


def _sigmoid(x):
    return 1.0 / (1.0 + jnp.exp(-x))


def _params(*sem):
    return pltpu.CompilerParams(dimension_semantics=sem, vmem_limit_bytes=VMEM_LIMIT_BYTES)


def _ffn_body(x_ref, g0_ref, g1_ref, wa_ref, wb_ref, wo_ref, o_ref, xn_ref, acc_ref, *, nf):
    f = pl.program_id(1)

    @pl.when(f == 0)
    def _():
        xn_ref[...] = _rms(x_ref[...], g0_ref[...]).astype(BF16)

    xn = xn_ref[...]
    a = jnp.dot(xn, wa_ref[...], preferred_element_type=F32)
    b = jnp.dot(xn, wb_ref[...], preferred_element_type=F32)
    h = (a * _sigmoid(a) * b).astype(BF16)
    contrib = jnp.dot(h, wo_ref[...], preferred_element_type=F32)

    @pl.when(f == 0)
    def _():
        acc_ref[...] = contrib

    @pl.when(f > 0)
    def _():
        acc_ref[...] += contrib

    @pl.when(f == nf - 1)
    def _():
        o_ref[...] = x_ref[...] + 0.5 * _rms(acc_ref[...], g1_ref[...])


def _ffn(x, g_pre, g_post, w_in, w_out, *, tm=ROW_TILE, tf=512):
    m, d = x.shape
    nf = D_FF // tf
    return pl.pallas_call(
        functools.partial(_ffn_body, nf=nf),
        grid=(m // tm, nf),
        in_specs=[
            pl.BlockSpec((tm, d), lambda i, f: (i, 0)),
            pl.BlockSpec((1, d), lambda i, f: (0, 0)),
            pl.BlockSpec((1, d), lambda i, f: (0, 0)),
            pl.BlockSpec((d, tf), lambda i, f: (0, f)),
            pl.BlockSpec((d, tf), lambda i, f: (0, f + nf)),
            pl.BlockSpec((tf, d), lambda i, f: (f, 0)),
        ],
        out_specs=pl.BlockSpec((tm, d), lambda i, f: (i, 0)),
        out_shape=jax.ShapeDtypeStruct((m, d), F32),
        scratch_shapes=[pltpu.VMEM((tm, d), BF16), pltpu.VMEM((tm, d), F32)],
        compiler_params=_params("parallel", "arbitrary"),
        name="ffn",
    )(x, g_pre, g_post, w_in, w_in, w_out)


def _proj_glu_body(x_ref, g_ref, wv_ref, wg_ref, wp_ref, a_ref, p_ref, xn_ref):
    @pl.when(pl.program_id(1) == 0)
    def _():
        xn_ref[...] = _rms(x_ref[...], g_ref[...]).astype(BF16)

    xn = xn_ref[...]
    val = jnp.dot(xn, wv_ref[...], preferred_element_type=F32)
    gate = jnp.dot(xn, wg_ref[...], preferred_element_type=F32)
    a_ref[...] = val * _sigmoid(gate)
    p_ref[...] = jnp.dot(xn, wp_ref[...], preferred_element_type=F32)


def _proj_glu(x, g, w, *, tm=ROW_TILE, tn=512):
    m, d = x.shape
    nj = D_CONV // tn
    return pl.pallas_call(
        _proj_glu_body,
        grid=(m // tm, nj),
        in_specs=[
            pl.BlockSpec((tm, d), lambda i, j: (i, 0)),
            pl.BlockSpec((1, d), lambda i, j: (0, 0)),
            pl.BlockSpec((d, tn), lambda i, j: (0, j)),
            pl.BlockSpec((d, tn), lambda i, j: (0, j + nj)),
            pl.BlockSpec((d, tn), lambda i, j: (0, j + 2 * nj)),
        ],
        out_specs=[
            pl.BlockSpec((tm, tn), lambda i, j: (i, j)),
            pl.BlockSpec((tm, tn), lambda i, j: (i, j)),
        ],
        out_shape=[jax.ShapeDtypeStruct((m, D_CONV), F32), jax.ShapeDtypeStruct((m, D_POOL), F32)],
        scratch_shapes=[pltpu.VMEM((tm, d), BF16)],
        compiler_params=_params("parallel", "arbitrary"),
        name="proj_glu",
    )(x, g, w, w, w)


def _proj_plain_body(x_ref, g_ref, w_ref, o_ref, xn_ref):
    @pl.when(pl.program_id(1) == 0)
    def _():
        xn_ref[...] = _rms(x_ref[...], g_ref[...]).astype(BF16)

    o_ref[...] = jnp.dot(xn_ref[...], w_ref[...], preferred_element_type=F32)


def _proj_plain(x, g, w, *, tm=ROW_TILE, tn=1024):
    m, d = x.shape
    n = w.shape[1]
    return pl.pallas_call(
        _proj_plain_body,
        grid=(m // tm, n // tn),
        in_specs=[
            pl.BlockSpec((tm, d), lambda i, j: (i, 0)),
            pl.BlockSpec((1, d), lambda i, j: (0, 0)),
            pl.BlockSpec((d, tn), lambda i, j: (0, j)),
        ],
        out_specs=pl.BlockSpec((tm, tn), lambda i, j: (i, j)),
        out_shape=jax.ShapeDtypeStruct((m, n), F32),
        scratch_shapes=[pltpu.VMEM((tm, d), BF16)],
        compiler_params=_params("parallel", "arbitrary"),
        name="proj_plain",
    )(x, g, w)


def _proj_out_body(y_ref, w_ref, x_ref, g_ref, o_ref):
    m = jnp.dot(y_ref[...], w_ref[...], preferred_element_type=F32)
    o_ref[...] = x_ref[...] + _rms(m, g_ref[...])


def _proj_out(y, w, x, g, *, tm=ROW_TILE):
    m, d = x.shape
    kdim = y.shape[1]
    return pl.pallas_call(
        _proj_out_body,
        grid=(m // tm,),
        in_specs=[
            pl.BlockSpec((tm, kdim), lambda i: (i, 0)),
            pl.BlockSpec((kdim, d), lambda i: (0, 0)),
            pl.BlockSpec((tm, d), lambda i: (i, 0)),
            pl.BlockSpec((1, d), lambda i: (0, 0)),
        ],
        out_specs=pl.BlockSpec((tm, d), lambda i: (i, 0)),
        out_shape=jax.ShapeDtypeStruct((m, d), F32),
        compiler_params=_params("parallel"),
        name="proj_out",
    )(y, w, x, g)


def _convpool_body(a_ref, ah_ref, ahist_ref, p_ref, ph_ref, phist_ref, cw_ref, cb_ref, lng_ref,
                   lnb_ref, pw_ref, ps_ref, o_ref, aext_ref, pext_ref, y_ref, *, t, rc, start_pos):
    i = pl.program_id(1)

    @pl.when(i == 0)
    def _():
        aext_ref[0:CONV_HALO, :] = ahist_ref[0]
        pext_ref[0:POOL_HALO, :] = phist_ref[0]

    @pl.when(i > 0)
    def _():
        aext_ref[0:CONV_HALO, :] = ah_ref[...]
        pext_ref[0:POOL_HALO, :] = ph_ref[...]

    aext_ref[CONV_HALO:CONV_HALO + t, :] = a_ref[...]
    pext_ref[POOL_HALO:POOL_HALO + t, :] = p_ref[...]

    lead = CONV_HALO - (CONV_WIDTH - 1)
    aext_ref[CONV_HALO + t:CONV_HALO + t + SUBLANES, :] = jnp.zeros((SUBLANES, D_CONV), F32)

    def conv_rows(r, carry):
        r0 = pl.multiple_of(r * rc, rc)
        for c in range(D_CONV // 128):
            cs = slice(c * 128, (c + 1) * 128)
            win = aext_ref[pl.ds(r0, rc + CONV_HALO + SUBLANES), cs]
            acc = jnp.zeros((rc, 128), F32)
            for ph in range(SUBLANES):
                shifted = win[ph:ph + rc + CONV_HALO]
                for q in range(CONV_HALO // SUBLANES + 1):
                    w = SUBLANES * q + ph - lead
                    if 0 <= w < CONV_WIDTH:
                        acc = acc + shifted[SUBLANES * q:SUBLANES * q + rc] * cw_ref[w:w + 1, cs]
            y_ref[pl.ds(r0, rc), cs] = acc + cb_ref[:, cs]
        return carry

    lax.fori_loop(0, t // rc, conv_rows, 0)

    y = y_ref[...]
    mu = jnp.mean(y, axis=-1, keepdims=True)
    yc = y - mu
    z = yc * lax.rsqrt(jnp.mean(yc * yc, axis=-1, keepdims=True) + EPS) * lng_ref[...] + lnb_ref[...]
    o_ref[:, 0:D_CONV] = (z * _sigmoid(z)).astype(BF16)

    pos = start_pos + i * t + lax.broadcasted_iota(jnp.int32, (t, 1), 0)
    for gi, win in enumerate(POOL_WINDOWS):
        cs = slice(gi * POOL_GROUP, (gi + 1) * POOL_GROUP)
        cur = pext_ref[POOL_HALO:POOL_HALO + t, cs]
        wsum = cur
        for dlt in range(1, win):
            wsum = wsum + pext_ref[POOL_HALO - dlt:POOL_HALO - dlt + t, cs]
        cnt = jnp.minimum(win, pos + 1).astype(F32)
        dev = (wsum / cnt - cur).astype(BF16)
        mixed = jnp.dot(dev, pw_ref[gi], preferred_element_type=F32) * ps_ref[:, cs]
        o_ref[:, D_CONV + gi * POOL_GROUP:D_CONV + (gi + 1) * POOL_GROUP] = mixed.astype(BF16)


def _convpool(a, p, a_hist, p_hist, cw, cb, lng, lnb, pw, ps, *, batch, seq, row_off, t, start_pos):
    nt = seq // t
    off_t, ca, cp = row_off // t, t // CONV_HALO, t // POOL_HALO
    if t >= CONV_HALO:
        ah_map = lambda b, i: (jnp.maximum((row_off // CONV_HALO) + (b * nt + i) * ca - 1, 0), 0)
        ph_map = lambda b, i: (jnp.maximum((row_off // POOL_HALO) + (b * nt + i) * cp - 1, 0), 0)
    else:
        assert nt == 1
        ah_map = lambda b, i: (0, 0)
        ph_map = lambda b, i: (0, 0)
    vec = lambda n: pl.BlockSpec((1, n), lambda b, i: (0, 0))
    rc = min(t, 64)
    return pl.pallas_call(
        functools.partial(_convpool_body, t=t, rc=rc, start_pos=start_pos),
        grid=(batch, nt),
        in_specs=[
            pl.BlockSpec((t, D_CONV), lambda b, i: (off_t + b * nt + i, 0)),
            pl.BlockSpec((CONV_HALO, D_CONV), ah_map),
            pl.BlockSpec((1, CONV_HALO, D_CONV), lambda b, i: (b, 0, 0)),
            pl.BlockSpec((t, D_POOL), lambda b, i: (off_t + b * nt + i, 0)),
            pl.BlockSpec((POOL_HALO, D_POOL), ph_map),
            pl.BlockSpec((1, POOL_HALO, D_POOL), lambda b, i: (b, 0, 0)),
            pl.BlockSpec((CONV_HALO, D_CONV), lambda b, i: (0, 0)),
            vec(D_CONV), vec(D_CONV), vec(D_CONV),
            pl.BlockSpec((len(POOL_WINDOWS), POOL_GROUP, POOL_GROUP), lambda b, i: (0, 0, 0)),
            vec(D_POOL),
        ],
        out_specs=pl.BlockSpec((t, D_CONV + D_POOL), lambda b, i: (b * nt + i, 0)),
        out_shape=jax.ShapeDtypeStruct((batch * seq, D_CONV + D_POOL), BF16),
        scratch_shapes=[
            pltpu.VMEM((CONV_HALO + t + SUBLANES, D_CONV), F32),
            pltpu.VMEM((POOL_HALO + t, D_POOL), F32),
            pltpu.VMEM((t, D_CONV), F32),
        ],
        compiler_params=_params("parallel", "arbitrary"),
        name="convpool",
    )(a, a, a_hist, p, p, p_hist, cw, cb, lng, lnb, pw, ps)


def _hgrn_body(q_ref, fz_ref, v_ref, gz_ref, lb_ref, gn_ref, s0_ref, o_ref, sout_ref,
               st_ref, g_ref, f_ref, *, c, nc):
    ci = pl.program_id(1)

    @pl.when(ci == 0)
    def _():
        for h in range(HEADS):
            st_ref[h] = s0_ref[0, h].T

    lb = lb_ref[...]
    f_all = lb + (1.0 - lb) * _sigmoid(fz_ref[...])
    f_ref[...] = f_all
    ri = lax.broadcasted_iota(jnp.int32, (c, c), 0)
    cj = lax.broadcasted_iota(jnp.int32, (c, c), 1)
    tril = (ri >= cj).astype(F32)
    g_ref[...] = jnp.dot(tril, jnp.log(f_all), precision=lax.Precision.HIGHEST,
                         preferred_element_type=F32)

    split = jnp.where(ri > cj, ri ^ cj, 0)
    row = lax.broadcasted_iota(jnp.int32, (c, 1), 0)
    levels = [1 << b for b in range(c.bit_length() - 1)]
    nt_dims = (((1,), (1,)), ((), ()))
    tn_dims = (((0,), (0,)), ((), ()))

    for h in range(HEADS):
        hs = slice(h * DK, (h + 1) * DK)
        qr = q_ref[:, hs]
        q = qr * _sigmoid(qr) * (DK ** -0.5)
        f = f_ref[:, hs]
        k = 1.0 - f
        v = v_ref[:, hs]
        g = g_ref[:, hs]
        vb = v.astype(BF16)

        att = jnp.zeros((c, c), F32)
        for s in levels:
            ls = s.bit_length() - 1
            second = ((row >> ls) & 1) == 1
            if s == 1:
                x = jnp.where(second, q * f, k)
            elif s == 2:
                f_prev = pltpu.roll(f, 1, 0)
                f_next = pltpu.roll(f, c - 1, 0)
                r4 = row & 3
                x = jnp.where(r4 == 0, k * f_next,
                              jnp.where(r4 == 1, k, jnp.where(r4 == 2, q * f, q * (f * f_prev))))
            else:
                pieces = []
                for pair in range(c // (2 * s)):
                    edge = pair * 2 * s + s - 1
                    pieces.append(jnp.broadcast_to(g_ref[edge:edge + 1, hs], (2 * s, DK)))
                ref = pieces[0] if len(pieces) == 1 else jnp.concatenate(pieces, axis=0)
                x = jnp.where(second, q, k) * jnp.exp(jnp.where(second, g - ref, ref - g))
            xb = x.astype(BF16)
            gram = lax.dot_general(xb, xb, nt_dims, preferred_element_type=F32)
            att = att + jnp.where((split >> ls) == 1, gram, 0.0)

        o = jnp.dot(att.astype(BF16), vb, preferred_element_type=F32)
        o = o + jnp.sum(q * k, axis=-1, keepdims=True) * v

        st = st_ref[h]
        qd = (q * jnp.exp(g)).astype(BF16)
        o = o + lax.dot_general(qd, st.astype(BF16), nt_dims, preferred_element_type=F32)

        g_last = g_ref[c - 1:c, hs]
        kd = (k * jnp.exp(g_last - g)).astype(BF16)
        st_ref[h] = st * jnp.exp(g_last) + lax.dot_general(vb, kd, tn_dims, preferred_element_type=F32)

        on = o * lax.rsqrt(jnp.mean(o * o, axis=-1, keepdims=True) + EPS) * gn_ref[...]
        gz = gz_ref[:, hs]
        o_ref[:, hs] = (on * (gz * _sigmoid(gz))).astype(BF16)

    @pl.when(ci == nc - 1)
    def _():
        for h in range(HEADS):
            sout_ref[0, h] = st_ref[h].T


def _hgrn(proj, lb, gnorm, s0, *, batch, seq, row_off, c):
    nc = seq // c
    off = row_off // c
    hd = HEADS * DK
    col = lambda j: pl.BlockSpec((c, hd), lambda b, i: (off + b * nc + i, j))
    state = pl.BlockSpec((1, HEADS, DK, DV), lambda b, i: (b, 0, 0, 0))
    return pl.pallas_call(
        functools.partial(_hgrn_body, c=c, nc=nc),
        grid=(batch, nc),
        in_specs=[
            col(0), col(1), col(2), col(3),
            pl.BlockSpec((1, hd), lambda b, i: (0, 0)),
            pl.BlockSpec((1, DV), lambda b, i: (0, 0)),
            state,
        ],
        out_specs=[pl.BlockSpec((c, hd), lambda b, i: (b * nc + i, 0)), state],
        out_shape=[
            jax.ShapeDtypeStruct((batch * seq, hd), BF16),
            jax.ShapeDtypeStruct((batch, HEADS, DK, DV), F32),
        ],
        scratch_shapes=[
            pltpu.VMEM((HEADS, DV, DK), F32),
            pltpu.VMEM((c, hd), F32),
            pltpu.VMEM((c, hd), F32),
        ],
        compiler_params=_params("parallel", "arbitrary"),
        name="hgrn",
    )(proj, proj, proj, proj, lb, gnorm, s0)


def _pad_hist(h, rows):
    b, n, d = h.shape
    return jnp.concatenate([jnp.zeros((b, rows - n, d), h.dtype), h], axis=1)


def kernel(x_prompt, x_sample, cache_conv, cache_pool, state_hgrn, ab_w_in, ab_w_out, conv_w, conv_b,
           conv_ln_g, conv_ln_b, pool_w, pool_scale, hgrn_w_in, hgrn_w_out, hgrn_gnorm, hgrn_lb, ffn_w_in,
           ffn_w_out, norm_g):
    bp, lp, d = x_prompt.shape
    bs, ls, _ = x_sample.shape
    mp, ms = bp * lp, bs * ls
    depth = ffn_w_in.shape[0]
    row = lambda vct: vct.reshape(1, -1).astype(F32)

    x = jnp.concatenate([x_prompt.reshape(mp, d), x_sample.reshape(ms, d)], axis=0)

    lb_all = jnp.cumsum(jax.nn.softmax(hgrn_lb.astype(F32), axis=0), axis=0)
    lb_all = lb_all - lb_all[0:1]

    ffn_w_in_b = ffn_w_in.astype(BF16)
    ffn_w_out_b = ffn_w_out.astype(BF16)

    conv_p, pool_p, hgrn_p, conv_s, pool_s, hgrn_s = [], [], [], [], [], []
    for l in range(depth):
        g = norm_g[l]
        x = _ffn(x, row(g[0]), row(g[1]), ffn_w_in_b[l, 0], ffn_w_out_b[l, 0])
        if l % 2 == 0:
            e = l // 2
            a, p = _proj_glu(x, row(g[2]), ab_w_in[e].astype(BF16))
            cw = jnp.concatenate([conv_w[e], jnp.zeros((CONV_HALO - CONV_WIDTH, D_CONV), F32)], axis=0)
            shared = (cw, row(conv_b[e]), row(conv_ln_g[e]), row(conv_ln_b[e]),
                      pool_w[e].astype(BF16), row(pool_scale[e]))
            hist_c, hist_p = cache_conv[e].astype(F32), cache_pool[e].astype(F32)
            zero_c = jnp.zeros((bp, CONV_WIDTH - 1, D_CONV), F32)
            zero_p = jnp.zeros((bp, POOL_MAX - 1, D_POOL), F32)
            y_p = _convpool(a, p, _pad_hist(zero_c, CONV_HALO), _pad_hist(zero_p, POOL_HALO), *shared,
                            batch=bp, seq=lp, row_off=0, t=512, start_pos=0)
            y_s = _convpool(a, p, _pad_hist(hist_c, CONV_HALO), _pad_hist(hist_p, POOL_HALO), *shared,
                            batch=bs, seq=ls, row_off=mp, t=ls, start_pos=PAST_LEN)
            y = jnp.concatenate([y_p, y_s], axis=0)
            w_out = ab_w_out[e].astype(BF16)
            a_p, a_s = a[:mp].reshape(bp, lp, D_CONV), a[mp:].reshape(bs, ls, D_CONV)
            p_p, p_s = p[:mp].reshape(bp, lp, D_POOL), p[mp:].reshape(bs, ls, D_POOL)
            conv_p.append(jnp.concatenate([zero_c, a_p], axis=1)[:, -(CONV_WIDTH - 1):])
            pool_p.append(jnp.concatenate([zero_p, p_p], axis=1)[:, -(POOL_MAX - 1):])
            conv_s.append(jnp.concatenate([hist_c, a_s], axis=1)[:, -(CONV_WIDTH - 1):])
            pool_s.append(jnp.concatenate([hist_p, p_s], axis=1)[:, -(POOL_MAX - 1):])
        else:
            o = l // 2
            proj = _proj_plain(x, row(g[2]), hgrn_w_in[o].astype(BF16))
            lb = row(lb_all[l])
            gn = row(hgrn_gnorm[o])
            zero_s = jnp.zeros((bp, HEADS, DK, DV), F32)
            y_p, s_p = _hgrn(proj, lb, gn, zero_s, batch=bp, seq=lp, row_off=0, c=128)
            y_s, s_s = _hgrn(proj, lb, gn, state_hgrn[o].astype(F32), batch=bs, seq=ls, row_off=mp, c=ls)
            y = jnp.concatenate([y_p, y_s], axis=0)
            w_out = hgrn_w_out[o].astype(BF16)
            hgrn_p.append(s_p)
            hgrn_s.append(s_s.astype(state_hgrn.dtype))
        x = _proj_out(y, w_out, x, row(g[3]))
        x = _ffn(x, row(g[4]), row(g[5]), ffn_w_in_b[l, 1], ffn_w_out_b[l, 1])

    return (x[:mp].reshape(bp, lp, d), x[mp:].reshape(bs, ls, d),
            jnp.stack(conv_p), jnp.stack(pool_p), jnp.stack(hgrn_p),
            jnp.stack(conv_s), jnp.stack(pool_s), jnp.stack(hgrn_s))
```

```python
import functools

import jax
import jax.numpy as jnp
from jax import lax
from jax.experimental import pallas as pl
from jax.experimental.pallas import tpu as pltpu

F32 = jnp.float32
BF16 = jnp.bfloat16

D_MODEL = 2048
D_CONV = 1024
D_POOL = 1024
CONV_WIDTH = 31
POOL_WINDOWS = (2, 4, 8, 16)
POOL_GROUP = 256
POOL_MAX = 16
HEADS = 16
DK = 128
DV = 128
D_FF = 5632
EPS = 1e-6
PAST_LEN = 4096

VMEM_LIMIT_BYTES = 54 * 1024 * 1024
ROW_TILE = 512
SUBLANES = 8
CONV_HALO = 32
POOL_HALO = 16
HEAD_SKEW = 2
NT_DIMS = (((1,), (1,)), ((), ()))
TN_DIMS = (((0,), (0,)), ((), ()))


def _rms(x, g):
    return x * lax.rsqrt(jnp.mean(x * x, axis=-1, keepdims=True) + EPS) * g


def _sigmoid(x):
    return 1.0 / (1.0 + jnp.exp(-x))


def _params(*sem):
    return pltpu.CompilerParams(dimension_semantics=sem, vmem_limit_bytes=VMEM_LIMIT_BYTES)


def _head_tail_specs(shape, n_head):
    head = pl.BlockSpec(shape, lambda i, *_: (jnp.minimum(i, n_head - 1), 0))
    tail = pl.BlockSpec(shape, lambda i, *_: (jnp.maximum(i - n_head, 0), 0))
    return head, tail


def _by_trunk(i, n_head, head_refs, tail_refs, fn):
    @pl.when(i < n_head)
    def _():
        fn(*head_refs)

    @pl.when(i >= n_head)
    def _():
        fn(*tail_refs)


def _ffn_body(*refs, nf, n_head, pair_in, pair_out):
    refs = list(refs)
    x_refs = [refs.pop(0) for _ in range(2 if pair_in else 1)]
    g0_ref, g1_ref, wa_ref, wb_ref, wo_ref = (refs.pop(0) for _ in range(5))
    o_refs = [refs.pop(0) for _ in range(2 if pair_out else 1)]
    xn_ref, acc_ref = refs
    i, f = pl.program_id(0), pl.program_id(1)

    def by_trunk(fn):
        if pair_in or pair_out:
            _by_trunk(i, n_head, (x_refs[0], o_refs[0]), (x_refs[-1], o_refs[-1]), fn)
        else:
            fn(x_refs[0], o_refs[0])

    @pl.when(f == 0)
    def _():
        def prologue(x_ref, _):
            xn_ref[...] = _rms(x_ref[...], g0_ref[...]).astype(BF16)

        by_trunk(prologue)
        acc_ref[...] = jnp.zeros(acc_ref.shape, F32)

    xn = xn_ref[...]
    a = jnp.dot(xn, wa_ref[...], preferred_element_type=F32)
    b = jnp.dot(xn, wb_ref[...], preferred_element_type=F32)
    h = (a * _sigmoid(a) * b).astype(BF16)
    acc_ref[...] += jnp.dot(h, wo_ref[...], preferred_element_type=F32)

    @pl.when(f == nf - 1)
    def _():
        def epilogue(x_ref, o_ref):
            o_ref[...] = x_ref[...] + 0.5 * _rms(acc_ref[...], g1_ref[...])

        by_trunk(epilogue)


def _ffn(x, g_pre, g_post, w_in, w_out, *, x_tail=None, split_rows=None, tm=ROW_TILE, tf=512):
    d = x.shape[1]
    m = x.shape[0] + (0 if x_tail is None else x_tail.shape[0])
    nf = D_FF // tf
    n_head = (x.shape[0] if x_tail is not None else (split_rows or m)) // tm
    row_spec = pl.BlockSpec((tm, d), lambda i, f: (i, 0))
    vec_spec = pl.BlockSpec((1, d), lambda i, f: (0, 0))
    x_specs = list(_head_tail_specs((tm, d), n_head)) if x_tail is not None else [row_spec]
    x_args = [x, x_tail] if x_tail is not None else [x]
    if split_rows is not None:
        out_specs = list(_head_tail_specs((tm, d), n_head))
        out_shape = [jax.ShapeDtypeStruct((split_rows, d), F32), jax.ShapeDtypeStruct((m - split_rows, d), F32)]
    else:
        out_specs, out_shape = row_spec, jax.ShapeDtypeStruct((m, d), F32)
    return pl.pallas_call(
        functools.partial(_ffn_body, nf=nf, n_head=n_head, pair_in=x_tail is not None,
                          pair_out=split_rows is not None),
        grid=(m // tm, nf),
        in_specs=x_specs + [
            vec_spec, vec_spec,
            pl.BlockSpec((d, tf), lambda i, f: (0, f)),
            pl.BlockSpec((d, tf), lambda i, f: (0, f + nf)),
            pl.BlockSpec((tf, d), lambda i, f: (f, 0)),
        ],
        out_specs=out_specs,
        out_shape=out_shape,
        scratch_shapes=[pltpu.VMEM((tm, d), BF16), pltpu.VMEM((tm, d), F32)],
        compiler_params=_params("parallel" if split_rows is None else "arbitrary", "arbitrary"),
        name="ffn",
    )(*x_args, g_pre, g_post, w_in, w_in, w_out)


def _proj_glu_body(x_ref, g_ref, wv_ref, wg_ref, wp_ref, a_ref, p_ref, xn_ref):
    @pl.when(pl.program_id(1) == 0)
    def _():
        xn_ref[...] = _rms(x_ref[...], g_ref[...]).astype(BF16)

    xn = xn_ref[...]
    val = jnp.dot(xn, wv_ref[...], preferred_element_type=F32)
    gate = jnp.dot(xn, wg_ref[...], preferred_element_type=F32)
    a_ref[...] = val * _sigmoid(gate)
    p_ref[...] = jnp.dot(xn, wp_ref[...], preferred_element_type=F32)


def _proj_glu(x, g, w, *, tm=ROW_TILE, tn=512):
    m, d = x.shape
    nj = D_CONV // tn
    return pl.pallas_call(
        _proj_glu_body,
        grid=(m // tm, nj),
        in_specs=[
            pl.BlockSpec((tm, d), lambda i, j: (i, 0)),
            pl.BlockSpec((1, d), lambda i, j: (0, 0)),
            pl.BlockSpec((d, tn), lambda i, j: (0, j)),
            pl.BlockSpec((d, tn), lambda i, j: (0, j + nj)),
            pl.BlockSpec((d, tn), lambda i, j: (0, j + 2 * nj)),
        ],
        out_specs=[
            pl.BlockSpec((tm, tn), lambda i, j: (i, j)),
            pl.BlockSpec((tm, tn), lambda i, j: (i, j)),
        ],
        out_shape=[jax.ShapeDtypeStruct((m, D_CONV), F32), jax.ShapeDtypeStruct((m, D_POOL), F32)],
        scratch_shapes=[pltpu.VMEM((tm, d), BF16)],
        compiler_params=_params("parallel", "arbitrary"),
        name="proj_glu",
    )(x, g, w, w, w)


def _proj_hgrn_body(x_ref, g_ref, lb_ref, wq_ref, wf_ref, wv_ref, wg_ref, q_ref, f_ref, v_ref, ga_ref, xn_ref):
    @pl.when(pl.program_id(1) == 0)
    def _():
        xn_ref[...] = _rms(x_ref[...], g_ref[...]).astype(BF16)

    xn = xn_ref[...]
    q = jnp.dot(xn, wq_ref[...], preferred_element_type=F32)
    q_ref[...] = (q * _sigmoid(q) * (DK ** -0.5)).astype(BF16)
    fz = jnp.dot(xn, wf_ref[...], preferred_element_type=F32)
    lb = lb_ref[...]
    f_ref[...] = lb + (1.0 - lb) * _sigmoid(fz)
    v_ref[...] = jnp.dot(xn, wv_ref[...], preferred_element_type=F32).astype(BF16)
    gz = jnp.dot(xn, wg_ref[...], preferred_element_type=F32)
    ga_ref[...] = (gz * _sigmoid(gz)).astype(BF16)


def _proj_hgrn(x, g, lb, w, *, tm=ROW_TILE, tn=512):
    m, d = x.shape
    hd = HEADS * DK
    nj = hd // tn
    wspec = lambda part: pl.BlockSpec((d, tn), lambda i, j: (0, j + part * nj))
    ospec = pl.BlockSpec((tm, tn), lambda i, j: (i, j))
    act = jax.ShapeDtypeStruct((m, hd), BF16)
    return pl.pallas_call(
        _proj_hgrn_body,
        grid=(m // tm, nj),
        in_specs=[
            pl.BlockSpec((tm, d), lambda i, j: (i, 0)),
            pl.BlockSpec((1, d), lambda i, j: (0, 0)),
            pl.BlockSpec((1, tn), lambda i, j: (0, j)),
            wspec(0), wspec(1), wspec(2), wspec(3),
        ],
        out_specs=[ospec, ospec, ospec, ospec],
        out_shape=[act, jax.ShapeDtypeStruct((m, hd), F32), act, act],
        scratch_shapes=[pltpu.VMEM((tm, d), BF16)],
        compiler_params=_params("parallel", "arbitrary"),
        name="proj_hgrn",
    )(x, g, lb, w, w, w, w)


def _proj_out_body(yh_ref, yt_ref, w_ref, x_ref, g_ref, o_ref, *, n_head):
    def project(y_ref):
        m = jnp.dot(y_ref[...], w_ref[...], preferred_element_type=F32)
        o_ref[...] = x_ref[...] + _rms(m, g_ref[...])

    _by_trunk(pl.program_id(0), n_head, (yh_ref,), (yt_ref,), project)


def _proj_out(y_head, y_tail, w, x, g, *, tm=ROW_TILE):
    m, d = x.shape
    kdim = w.shape[0]
    n_head = y_head.shape[0] // tm
    return pl.pallas_call(
        functools.partial(_proj_out_body, n_head=n_head),
        grid=(m // tm,),
        in_specs=list(_head_tail_specs((tm, kdim), n_head)) + [
            pl.BlockSpec((kdim, d), lambda i: (0, 0)),
            pl.BlockSpec((tm, d), lambda i: (i, 0)),
            pl.BlockSpec((1, d), lambda i: (0, 0)),
        ],
        out_specs=pl.BlockSpec((tm, d), lambda i: (i, 0)),
        out_shape=jax.ShapeDtypeStruct((m, d), F32),
        compiler_params=_params("parallel"),
        name="proj_out",
    )(y_head, y_tail, w, x, g)


def _convpool_body(a_ref, ah_ref, ahist_ref, p_ref, ph_ref, phist_ref, cw_ref, cb_ref, lng_ref,
                   lnb_ref, pw_ref, ps_ref, o_ref, aext_ref, pext_ref, y_ref, *, t, rc, start_pos):
    i = pl.program_id(1)

    @pl.when(i == 0)
    def _():
        aext_ref[0:CONV_HALO, :] = ahist_ref[0]
        pext_ref[0:POOL_HALO, :] = phist_ref[0]

    @pl.when(i > 0)
    def _():
        aext_ref[0:CONV_HALO, :] = ah_ref[...]
        pext_ref[0:POOL_HALO, :] = ph_ref[...]

    aext_ref[CONV_HALO:CONV_HALO + t, :] = a_ref[...]
    aext_ref[CONV_HALO + t:CONV_HALO + t + SUBLANES, :] = jnp.zeros((SUBLANES, D_CONV), F32)
    pext_ref[POOL_HALO:POOL_HALO + t, :] = p_ref[...]

    lead = CONV_HALO - (CONV_WIDTH - 1)

    def conv_rows(r, carry):
        r0 = pl.multiple_of(r * rc, rc)
        for c in range(D_CONV // 128):
            cs = slice(c * 128, (c + 1) * 128)
            win = aext_ref[pl.ds(r0, rc + CONV_HALO + SUBLANES), cs]
            acc = None
            for ph in range(SUBLANES):
                part = None
                for q in range(CONV_HALO // SUBLANES + 1):
                    w = SUBLANES * q + ph - lead
                    if 0 <= w < CONV_WIDTH:
                        term = win[SUBLANES * q:SUBLANES * q + rc + SUBLANES] * cw_ref[w:w + 1, cs]
                        part = term if part is None else part + term
                shifted = part[ph:ph + rc]
                acc = shifted if acc is None else acc + shifted
            y_ref[pl.ds(r0, rc), cs] = acc + cb_ref[:, cs]
        return carry

    lax.fori_loop(0, t // rc, conv_rows, 0)

    y = y_ref[...]
    mu = jnp.mean(y, axis=-1, keepdims=True)
    yc = y - mu
    z = yc * lax.rsqrt(jnp.mean(yc * yc, axis=-1, keepdims=True) + EPS) * lng_ref[...] + lnb_ref[...]
    o_ref[:, 0:D_CONV] = (z * _sigmoid(z)).astype(BF16)

    pos = start_pos + i * t + lax.broadcasted_iota(jnp.int32, (t, 1), 0)
    for gi, win_len in enumerate(POOL_WINDOWS):
        cs = slice(gi * POOL_GROUP, (gi + 1) * POOL_GROUP)
        cur = pext_ref[POOL_HALO:POOL_HALO + t, cs]
        wsum = cur
        for dlt in range(1, win_len):
            wsum = wsum + pext_ref[POOL_HALO - dlt:POOL_HALO - dlt + t, cs]
        cnt = jnp.minimum(win_len, pos + 1).astype(F32)
        dev = (wsum / cnt - cur).astype(BF16)
        mixed = jnp.dot(dev, pw_ref[gi], preferred_element_type=F32) * ps_ref[:, cs]
        o_ref[:, D_CONV + gi * POOL_GROUP:D_CONV + (gi + 1) * POOL_GROUP] = mixed.astype(BF16)


def _convpool(a, p, a_hist, p_hist, cw, cb, lng, lnb, pw, ps, *, batch, seq, row_off, t, start_pos):
    nt = seq // t
    off_t, ca, cp = row_off // t, t // CONV_HALO, t // POOL_HALO
    if t >= CONV_HALO:
        ah_map = lambda b, i: (jnp.maximum((row_off // CONV_HALO) + (b * nt + i) * ca - 1, 0), 0)
        ph_map = lambda b, i: (jnp.maximum((row_off // POOL_HALO) + (b * nt + i) * cp - 1, 0), 0)
    else:
        assert nt == 1
        ah_map = lambda b, i: (0, 0)
        ph_map = lambda b, i: (0, 0)
    vec = lambda n: pl.BlockSpec((1, n), lambda b, i: (0, 0))
    rc = min(t, 64)
    return pl.pallas_call(
        functools.partial(_convpool_body, t=t, rc=rc, start_pos=start_pos),
        grid=(batch, nt),
        in_specs=[
            pl.BlockSpec((t, D_CONV), lambda b, i: (off_t + b * nt + i, 0)),
            pl.BlockSpec((CONV_HALO, D_CONV), ah_map),
            pl.BlockSpec((1, CONV_HALO, D_CONV), lambda b, i: (b, 0, 0)),
            pl.BlockSpec((t, D_POOL), lambda b, i: (off_t + b * nt + i, 0)),
            pl.BlockSpec((POOL_HALO, D_POOL), ph_map),
            pl.BlockSpec((1, POOL_HALO, D_POOL), lambda b, i: (b, 0, 0)),
            pl.BlockSpec((CONV_HALO, D_CONV), lambda b, i: (0, 0)),
            vec(D_CONV), vec(D_CONV), vec(D_CONV),
            pl.BlockSpec((len(POOL_WINDOWS), POOL_GROUP, POOL_GROUP), lambda b, i: (0, 0, 0)),
            vec(D_POOL),
        ],
        out_specs=pl.BlockSpec((t, D_CONV + D_POOL), lambda b, i: (b * nt + i, 0)),
        out_shape=jax.ShapeDtypeStruct((batch * seq, D_CONV + D_POOL), BF16),
        scratch_shapes=[
            pltpu.VMEM((CONV_HALO + t + SUBLANES, D_CONV), F32),
            pltpu.VMEM((POOL_HALO + t, D_POOL), F32),
            pltpu.VMEM((t, D_CONV), F32),
        ],
        compiler_params=_params("parallel", "arbitrary"),
        name="convpool",
    )(a, a, a_hist, p, p, p_hist, cw, cb, lng, lnb, pw, ps)


def _hgrn_body(q_ref, f_ref, v_ref, ga_ref, gn_ref, s0_ref, o_ref, sout_ref,
               st_ref, g_ref, m_ref, *, c, nc):
    ci = pl.program_id(1)
    levels = [1 << b for b in range(c.bit_length() - 1)]
    ri = lax.broadcasted_iota(jnp.int32, (c, c), 0)
    cj = lax.broadcasted_iota(jnp.int32, (c, c), 1)

    @pl.when(ci == 0)
    def _():
        for h in range(HEADS):
            st_ref[h] = s0_ref[0, h].T
        split = jnp.where(ri > cj, ri ^ cj, 0)
        for li in range(len(levels)):
            m_ref[li] = ((split >> li) == 1).astype(F32)

    logf = jnp.log(f_ref[...])
    hi = logf.astype(BF16)
    rest = logf - hi.astype(F32)
    mid = rest.astype(BF16)
    lo = (rest - mid.astype(F32)).astype(BF16)
    tril = (ri >= cj).astype(BF16)
    g_ref[...] = (jnp.dot(tril, hi, preferred_element_type=F32)
                  + jnp.dot(tril, mid, preferred_element_type=F32)
                  + jnp.dot(tril, lo, preferred_element_type=F32))

    n = c // SUBLANES
    sub = lax.broadcasted_iota(jnp.int32, (1, SUBLANES, DK), 1)
    tiles = lambda arr: arr.reshape(n, SUBLANES, DK)

    def attend(h):
        hs = slice(h * DK, (h + 1) * DK)
        q = q_ref[:, hs].astype(F32)
        f = f_ref[:, hs]
        k = 1.0 - f
        vb = v_ref[:, hs]
        v = vb.astype(F32)
        g = g_ref[:, hs]
        q3, k3, f3, g3 = tiles(q), tiles(k), tiles(f), tiles(g)
        qf3 = q3 * f3

        xs = []
        for s in levels:
            if s == 1:
                x = jnp.where((sub & 1) == 1, qf3, k3)
            elif s == 2:
                f_prev = pltpu.roll(f3, 1, 1)
                f_next = pltpu.roll(f3, SUBLANES - 1, 1)
                r4 = sub & 3
                x = jnp.where(r4 == 0, k3 * f_next,
                              jnp.where(r4 == 1, k3, jnp.where(r4 == 2, qf3, qf3 * f_prev)))
            elif s == 4:
                ref = jnp.broadcast_to(g3[:, 3:4, :], g3.shape)
                upper = (sub & 4) != 0
                x = jnp.where(upper, q3, k3) * jnp.exp(jnp.where(upper, g3 - ref, ref - g3))
            else:
                pieces = []
                for pair in range(c // (2 * s)):
                    b0 = pair * 2 * s
                    ref = g_ref[b0 + s - 1:b0 + s, hs]
                    pieces.append(k[b0:b0 + s] * jnp.exp(ref - g[b0:b0 + s]))
                    pieces.append(q[b0 + s:b0 + 2 * s] * jnp.exp(g[b0 + s:b0 + 2 * s] - ref))
                x = jnp.concatenate(pieces, axis=0)
            xs.append(x.reshape(c, DK).astype(BF16))

        att = None
        for li, xb in enumerate(xs):
            term = lax.dot_general(xb, xb, NT_DIMS, preferred_element_type=F32) * m_ref[li]
            att = term if att is None else att + term
        return q, k, v, vb, g, att.astype(BF16)

    def finish(h, q, k, v, vb, g, att):
        hs = slice(h * DK, (h + 1) * DK)
        o = jnp.dot(att, vb, preferred_element_type=F32)
        o = o + jnp.sum(q * k, axis=-1, keepdims=True) * v

        st = st_ref[h]
        qd = (q * jnp.exp(g)).astype(BF16)
        o = o + lax.dot_general(qd, st.astype(BF16), NT_DIMS, preferred_element_type=F32)

        g_last = g_ref[c - 1:c, hs]
        kd = (k * jnp.exp(g_last - g)).astype(BF16)
        st_ref[h] = st * jnp.exp(g_last) + lax.dot_general(vb, kd, TN_DIMS, preferred_element_type=F32)

        on = o * lax.rsqrt(jnp.mean(o * o, axis=-1, keepdims=True) + EPS) * gn_ref[...]
        o_ref[:, hs] = (on * ga_ref[:, hs].astype(F32)).astype(BF16)

    pending = []
    for h in range(HEADS):
        pending.append((h, attend(h)))
        if len(pending) > HEAD_SKEW:
            done, vals = pending.pop(0)
            finish(done, *vals)
    for done, vals in pending:
        finish(done, *vals)

    @pl.when(ci == nc - 1)
    def _():
        for h in range(HEADS):
            sout_ref[0, h] = st_ref[h].T


def _hgrn(q, f, v, ga, gnorm, s0, *, batch, seq, row_off, c):
    nc = seq // c
    off = row_off // c
    hd = HEADS * DK
    rows = pl.BlockSpec((c, hd), lambda b, i: (off + b * nc + i, 0))
    state = pl.BlockSpec((1, HEADS, DK, DV), lambda b, i: (b, 0, 0, 0))
    n_levels = c.bit_length() - 1
    return pl.pallas_call(
        functools.partial(_hgrn_body, c=c, nc=nc),
        grid=(batch, nc),
        in_specs=[rows, rows, rows, rows, pl.BlockSpec((1, DV), lambda b, i: (0, 0)), state],
        out_specs=[pl.BlockSpec((c, hd), lambda b, i: (b * nc + i, 0)), state],
        out_shape=[
            jax.ShapeDtypeStruct((batch * seq, hd), BF16),
            jax.ShapeDtypeStruct((batch, HEADS, DK, DV), F32),
        ],
        scratch_shapes=[
            pltpu.VMEM((HEADS, DV, DK), F32),
            pltpu.VMEM((c, hd), F32),
            pltpu.VMEM((n_levels, c, c), F32),
        ],
        compiler_params=_params("parallel", "arbitrary"),
        name="hgrn",
    )(q, f, v, ga, gnorm, s0)


def _pad_hist(h, rows):
    b, n, d = h.shape
    return jnp.concatenate([jnp.zeros((b, rows - n, d), h.dtype), h], axis=1)


def _new_cache(hist, rows, batch, seq, off, keep):
    width = rows.shape[1]
    if seq >= keep:
        return rows[off:off + batch * seq].reshape(batch, seq, width)[:, seq - keep:]
    cur = rows[off:off + batch * seq].reshape(batch, seq, width)
    return jnp.concatenate([hist[:, seq:], cur], axis=1)


def kernel(x_prompt, x_sample, cache_conv, cache_pool, state_hgrn, ab_w_in, ab_w_out, conv_w, conv_b,
           conv_ln_g, conv_ln_b, pool_w, pool_scale, hgrn_w_in, hgrn_w_out, hgrn_gnorm, hgrn_lb, ffn_w_in,
           ffn_w_out, norm_g):
    bp, lp, d = x_prompt.shape
    bs, ls, _ = x_sample.shape
    mp, ms = bp * lp, bs * ls
    depth = ffn_w_in.shape[0]
    row = lambda vct: vct.reshape(1, -1).astype(F32)

    lb_all = jnp.cumsum(jax.nn.softmax(hgrn_lb.astype(F32), axis=0), axis=0)
    lb_all = lb_all - lb_all[0:1]

    ffn_w_in_b = ffn_w_in.astype(BF16)
    ffn_w_out_b = ffn_w_out.astype(BF16)

    conv_p, pool_p, hgrn_p, conv_s, pool_s, hgrn_s = [], [], [], [], [], []
    x = None
    for l in range(depth):
        g = norm_g[l]
        if l == 0:
            x = _ffn(x_prompt.reshape(mp, d), row(g[0]), row(g[1]), ffn_w_in_b[l, 0], ffn_w_out_b[l, 0],
                     x_tail=x_sample.reshape(ms, d))
        else:
            x = _ffn(x, row(g[0]), row(g[1]), ffn_w_in_b[l, 0], ffn_w_out_b[l, 0])
        if l % 2 == 0:
            e = l // 2
            a, p = _proj_glu(x, row(g[2]), ab_w_in[e].astype(BF16))
            cw = jnp.concatenate([conv_w[e], jnp.zeros((CONV_HALO - CONV_WIDTH, D_CONV), F32)], axis=0)
            shared = (cw, row(conv_b[e]), row(conv_ln_g[e]), row(conv_ln_b[e]),
                      pool_w[e].astype(BF16), row(pool_scale[e]))
            hist_c, hist_p = cache_conv[e].astype(F32), cache_pool[e].astype(F32)
            zero_c = jnp.zeros((bp, CONV_WIDTH - 1, D_CONV), F32)
            zero_p = jnp.zeros((bp, POOL_MAX - 1, D_POOL), F32)
            y_p = _convpool(a, p, _pad_hist(zero_c, CONV_HALO), _pad_hist(zero_p, POOL_HALO), *shared,
                            batch=bp, seq=lp, row_off=0, t=512, start_pos=0)
            y_s = _convpool(a, p, _pad_hist(hist_c, CONV_HALO), _pad_hist(hist_p, POOL_HALO), *shared,
                            batch=bs, seq=ls, row_off=mp, t=ls, start_pos=PAST_LEN)
            w_out = ab_w_out[e].astype(BF16)
            conv_p.append(_new_cache(zero_c, a, bp, lp, 0, CONV_WIDTH - 1))
            pool_p.append(_new_cache(zero_p, p, bp, lp, 0, POOL_MAX - 1))
            conv_s.append(_new_cache(hist_c, a, bs, ls, mp, CONV_WIDTH - 1))
            pool_s.append(_new_cache(hist_p, p, bs, ls, mp, POOL_MAX - 1))
        else:
            o = l // 2
            q, f, v, ga = _proj_hgrn(x, row(g[2]), row(lb_all[l]), hgrn_w_in[o].astype(BF16))
            gn = row(hgrn_gnorm[o])
            zero_s = jnp.zeros((bp, HEADS, DK, DV), F32)
            y_p, s_p = _hgrn(q, f, v, ga, gn, zero_s, batch=bp, seq=lp, row_off=0, c=128)
            y_s, s_s = _hgrn(q, f, v, ga, gn, state_hgrn[o].astype(F32), batch=bs, seq=ls, row_off=mp, c=ls)
            w_out = hgrn_w_out[o].astype(BF16)
            hgrn_p.append(s_p)
            hgrn_s.append(s_s.astype(state_hgrn.dtype))
        x = _proj_out(y_p, y_s, w_out, x, row(g[3]))
        if l == depth - 1:
            x_p, x_s = _ffn(x, row(g[4]), row(g[5]), ffn_w_in_b[l, 1], ffn_w_out_b[l, 1], split_rows=mp)
        else:
            x = _ffn(x, row(g[4]), row(g[5]), ffn_w_in_b[l, 1], ffn_w_out_b[l, 1])

    return (x_p.reshape(bp, lp, d), x_s.reshape(bs, ls, d),
            jnp.stack(conv_p), jnp.stack(pool_p), jnp.stack(hgrn_p),
            jnp.stack(conv_s), jnp.stack(pool_s), jnp.stack(hgrn_s))
```

```python
import functools

import jax
import jax.numpy as jnp
from jax import lax
from jax.experimental import pallas as pl
from jax.experimental.pallas import tpu as pltpu

F32 = jnp.float32
BF16 = jnp.bfloat16

D_MODEL = 2048
D_CONV = 1024
D_POOL = 1024
CONV_WIDTH = 31
POOL_WINDOWS = (2, 4, 8, 16)
POOL_GROUP = 256
POOL_MAX = 16
HEADS = 16
DK = 128
DV = 128
D_FF = 5632
EPS = 1e-6
PAST_LEN = 4096
LOG2_E = 1.4426950408889634

VMEM_LIMIT_BYTES = 54 * 1024 * 1024
ROW_TILE = 512
SUBLANES = 8
CONV_HALO = 32
POOL_HALO = 16
HEAD_SKEW = 2
NT_DIMS = (((1,), (1,)), ((), ()))
TN_DIMS = (((0,), (0,)), ((), ()))


def _rms(x, g):
    return x * lax.rsqrt(jnp.mean(x * x, axis=-1, keepdims=True) + EPS) * g


def _sigmoid(x):
    return 1.0 / (1.0 + jnp.exp(-x))


def _params(*sem):
    return pltpu.CompilerParams(dimension_semantics=sem, vmem_limit_bytes=VMEM_LIMIT_BYTES)


def _head_tail_specs(shape, n_head):
    head = pl.BlockSpec(shape, lambda i, *_: (jnp.minimum(i, n_head - 1), 0))
    tail = pl.BlockSpec(shape, lambda i, *_: (jnp.maximum(i - n_head, 0), 0))
    return head, tail


def _by_trunk(i, n_head, head_refs, tail_refs, fn):
    @pl.when(i < n_head)
    def _():
        fn(*head_refs)

    @pl.when(i >= n_head)
    def _():
        fn(*tail_refs)


def _ffn_body(*refs, nf, n_head, pair_in, pair_out):
    refs = list(refs)
    x_refs = [refs.pop(0) for _ in range(2 if pair_in else 1)]
    g0_ref, g1_ref, wa_ref, wb_ref, wo_ref = (refs.pop(0) for _ in range(5))
    o_refs = [refs.pop(0) for _ in range(2 if pair_out else 1)]
    xn_ref, acc_ref = refs
    i, f = pl.program_id(0), pl.program_id(1)

    def by_trunk(fn):
        if pair_in or pair_out:
            _by_trunk(i, n_head, (x_refs[0], o_refs[0]), (x_refs[-1], o_refs[-1]), fn)
        else:
            fn(x_refs[0], o_refs[0])

    @pl.when(f == 0)
    def _():
        def prologue(x_ref, _):
            xn_ref[...] = _rms(x_ref[...], g0_ref[...]).astype(BF16)

        by_trunk(prologue)
        acc_ref[...] = jnp.zeros(acc_ref.shape, F32)

    xn = xn_ref[...]
    a = jnp.dot(xn, wa_ref[...], preferred_element_type=F32)
    b = jnp.dot(xn, wb_ref[...], preferred_element_type=F32)
    h = (a * _sigmoid(a) * b).astype(BF16)
    acc_ref[...] += jnp.dot(h, wo_ref[...], preferred_element_type=F32)

    @pl.when(f == nf - 1)
    def _():
        def epilogue(x_ref, o_ref):
            o_ref[...] = x_ref[...] + _rms(acc_ref[...], g1_ref[...])

        by_trunk(epilogue)


def _ffn(x, g_pre, g_post, w_in, w_out, widx, *, x_tail=None, split_rows=None, tm=ROW_TILE, tf=512):
    wl, wj = widx
    d = x.shape[1]
    m = x.shape[0] + (0 if x_tail is None else x_tail.shape[0])
    nf = D_FF // tf
    n_head = (x.shape[0] if x_tail is not None else (split_rows or m)) // tm
    row_spec = pl.BlockSpec((tm, d), lambda i, f: (i, 0))
    vec_spec = pl.BlockSpec((1, d), lambda i, f: (0, 0))
    x_specs = list(_head_tail_specs((tm, d), n_head)) if x_tail is not None else [row_spec]
    x_args = [x, x_tail] if x_tail is not None else [x]
    if split_rows is not None:
        out_specs = list(_head_tail_specs((tm, d), n_head))
        out_shape = [jax.ShapeDtypeStruct((split_rows, d), F32), jax.ShapeDtypeStruct((m - split_rows, d), F32)]
    else:
        out_specs, out_shape = row_spec, jax.ShapeDtypeStruct((m, d), F32)
    return pl.pallas_call(
        functools.partial(_ffn_body, nf=nf, n_head=n_head, pair_in=x_tail is not None,
                          pair_out=split_rows is not None),
        grid=(m // tm, nf),
        in_specs=x_specs + [
            vec_spec, vec_spec,
            pl.BlockSpec((None, None, d, tf), lambda i, f: (wl, wj, 0, f)),
            pl.BlockSpec((None, None, d, tf), lambda i, f: (wl, wj, 0, f + nf)),
            pl.BlockSpec((None, None, tf, d), lambda i, f: (wl, wj, f, 0)),
        ],
        out_specs=out_specs,
        out_shape=out_shape,
        scratch_shapes=[pltpu.VMEM((tm, d), BF16), pltpu.VMEM((tm, d), F32)],
        compiler_params=_params("parallel" if split_rows is None else "arbitrary", "arbitrary"),
        name="ffn",
    )(*x_args, g_pre, g_post, w_in, w_in, w_out)


def _proj_glu_body(x_ref, g_ref, wv_ref, wg_ref, wp_ref, a_ref, p_ref):
    xn = _rms(x_ref[...], g_ref[...]).astype(BF16)
    val = jnp.dot(xn, wv_ref[...], preferred_element_type=F32)
    gate = jnp.dot(xn, wg_ref[...], preferred_element_type=F32)
    a_ref[...] = val * _sigmoid(gate)
    p_ref[...] = jnp.dot(xn, wp_ref[...], preferred_element_type=F32)


def _proj_glu(x, g, w, e, *, tm=ROW_TILE):
    m, d = x.shape
    wspec = lambda part: pl.BlockSpec((None, d, D_CONV), lambda i: (e, 0, part),
                                      pipeline_mode=pl.Buffered(1))
    return pl.pallas_call(
        _proj_glu_body,
        grid=(m // tm,),
        in_specs=[
            pl.BlockSpec((tm, d), lambda i: (i, 0)),
            pl.BlockSpec((1, d), lambda i: (0, 0)),
            wspec(0), wspec(1), wspec(2),
        ],
        out_specs=[
            pl.BlockSpec((tm, D_CONV), lambda i: (i, 0)),
            pl.BlockSpec((tm, D_POOL), lambda i: (i, 0)),
        ],
        out_shape=[jax.ShapeDtypeStruct((m, D_CONV), F32), jax.ShapeDtypeStruct((m, D_POOL), F32)],
        compiler_params=_params("parallel"),
        name="proj_glu",
    )(x, g, w, w, w)


def _proj_hgrn_body(x_ref, g_ref, lb_ref, wq_ref, wf_ref, wv_ref, wg_ref, q_ref, f_ref, v_ref, ga_ref, xn_ref):
    @pl.when(pl.program_id(1) == 0)
    def _():
        xn_ref[...] = _rms(x_ref[...], g_ref[...]).astype(BF16)

    xn = xn_ref[...]
    q = jnp.dot(xn, wq_ref[...], preferred_element_type=F32)
    q_ref[...] = (q * _sigmoid(q) * (DK ** -0.5)).astype(BF16)
    fz = jnp.dot(xn, wf_ref[...], preferred_element_type=F32)
    lb = lb_ref[...]
    f_ref[...] = lb + (1.0 - lb) * _sigmoid(fz)
    v_ref[...] = jnp.dot(xn, wv_ref[...], preferred_element_type=F32).astype(BF16)
    gz = jnp.dot(xn, wg_ref[...], preferred_element_type=F32)
    ga_ref[...] = (gz * _sigmoid(gz)).astype(BF16)


def _proj_hgrn(x, g, lb, w, o, *, tm=ROW_TILE, tn=512):
    m, d = x.shape
    hd = HEADS * DK
    nj = hd // tn
    wspec = lambda part: pl.BlockSpec((None, d, tn), lambda i, j: (o, 0, j + part * nj))
    ospec = pl.BlockSpec((tm, tn), lambda i, j: (i, j))
    act = jax.ShapeDtypeStruct((m, hd), BF16)
    return pl.pallas_call(
        _proj_hgrn_body,
        grid=(m // tm, nj),
        in_specs=[
            pl.BlockSpec((tm, d), lambda i, j: (i, 0)),
            pl.BlockSpec((1, d), lambda i, j: (0, 0)),
            pl.BlockSpec((1, tn), lambda i, j: (0, j)),
            wspec(0), wspec(1), wspec(2), wspec(3),
        ],
        out_specs=[ospec, ospec, ospec, ospec],
        out_shape=[act, jax.ShapeDtypeStruct((m, hd), F32), act, act],
        scratch_shapes=[pltpu.VMEM((tm, d), BF16)],
        compiler_params=_params("parallel", "arbitrary"),
        name="proj_hgrn",
    )(x, g, lb, w, w, w, w)


def _proj_out_body(yh_ref, yt_ref, w_ref, x_ref, g_ref, o_ref, *, n_head):
    def project(y_ref):
        m = jnp.dot(y_ref[...], w_ref[...], preferred_element_type=F32)
        o_ref[...] = x_ref[...] + _rms(m, g_ref[...])

    _by_trunk(pl.program_id(0), n_head, (yh_ref,), (yt_ref,), project)


def _proj_out(y_head, y_tail, w, e, x, g, *, tm=ROW_TILE):
    m, d = x.shape
    kdim = w.shape[1]
    n_head = y_head.shape[0] // tm
    return pl.pallas_call(
        functools.partial(_proj_out_body, n_head=n_head),
        grid=(m // tm,),
        in_specs=list(_head_tail_specs((tm, kdim), n_head)) + [
            pl.BlockSpec((None, kdim, d), lambda i: (e, 0, 0), pipeline_mode=pl.Buffered(1)),
            pl.BlockSpec((tm, d), lambda i: (i, 0)),
            pl.BlockSpec((1, d), lambda i: (0, 0)),
        ],
        out_specs=pl.BlockSpec((tm, d), lambda i: (i, 0)),
        out_shape=jax.ShapeDtypeStruct((m, d), F32),
        compiler_params=_params("parallel"),
        name="proj_out",
    )(y_head, y_tail, w, x, g)


def _convpool_body(a_ref, ah_ref, ahist_ref, p_ref, ph_ref, phist_ref, cw_ref, cb_ref, lng_ref,
                   lnb_ref, pw_ref, ps_ref, o_ref, aext_ref, pext_ref, y_ref, *, t, rc, start_pos):
    i = pl.program_id(1)

    @pl.when(i == 0)
    def _():
        aext_ref[0:CONV_HALO, :] = ahist_ref[0]
        pext_ref[0:POOL_HALO, :] = phist_ref[0]

    @pl.when(i > 0)
    def _():
        aext_ref[0:CONV_HALO, :] = ah_ref[...]
        pext_ref[0:POOL_HALO, :] = ph_ref[...]

    aext_ref[CONV_HALO:CONV_HALO + t, :] = a_ref[...]
    aext_ref[CONV_HALO + t:CONV_HALO + t + SUBLANES, :] = jnp.zeros((SUBLANES, D_CONV), F32)
    pext_ref[POOL_HALO:POOL_HALO + t, :] = p_ref[...]

    lead = CONV_HALO - (CONV_WIDTH - 1)

    def conv_rows(r, carry):
        r0 = pl.multiple_of(r * rc, rc)
        for c in range(D_CONV // 128):
            cs = slice(c * 128, (c + 1) * 128)
            win = aext_ref[pl.ds(r0, rc + CONV_HALO + SUBLANES), cs]
            acc = None
            for ph in range(SUBLANES):
                part = None
                for q in range(CONV_HALO // SUBLANES + 1):
                    w = SUBLANES * q + ph - lead
                    if 0 <= w < CONV_WIDTH:
                        term = win[SUBLANES * q:SUBLANES * q + rc + SUBLANES] * cw_ref[w:w + 1, cs]
                        part = term if part is None else part + term
                shifted = part[ph:ph + rc]
                acc = shifted if acc is None else acc + shifted
            y_ref[pl.ds(r0, rc), cs] = acc + cb_ref[:, cs]
        return carry

    lax.fori_loop(0, t // rc, conv_rows, 0)

    y = y_ref[...]
    mu = jnp.mean(y, axis=-1, keepdims=True)
    yc = y - mu
    z = yc * lax.rsqrt(jnp.mean(yc * yc, axis=-1, keepdims=True) + EPS) * lng_ref[...] + lnb_ref[...]
    o_ref[:, 0:D_CONV] = (z * _sigmoid(z)).astype(BF16)

    pos = start_pos + i * t + lax.broadcasted_iota(jnp.int32, (t, 1), 0)
    for gi, win_len in enumerate(POOL_WINDOWS):
        cs = slice(gi * POOL_GROUP, (gi + 1) * POOL_GROUP)
        cur = pext_ref[POOL_HALO:POOL_HALO + t, cs]
        wsum = cur
        for dlt in range(1, win_len):
            wsum = wsum + pext_ref[POOL_HALO - dlt:POOL_HALO - dlt + t, cs]
        cnt = jnp.minimum(win_len, pos + 1).astype(F32)
        dev = (wsum / cnt - cur).astype(BF16)
        mixed = jnp.dot(dev, pw_ref[gi], preferred_element_type=F32) * ps_ref[:, cs]
        o_ref[:, D_CONV + gi * POOL_GROUP:D_CONV + (gi + 1) * POOL_GROUP] = mixed.astype(BF16)


def _convpool(a, p, a_hist, p_hist, cw, cb, lng, lnb, pw, ps, *, e, batch, seq, row_off, t, start_pos):
    nt = seq // t
    off_t, ca, cp = row_off // t, t // CONV_HALO, t // POOL_HALO
    if t >= CONV_HALO:
        ah_map = lambda b, i: (jnp.maximum((row_off // CONV_HALO) + (b * nt + i) * ca - 1, 0), 0)
        ph_map = lambda b, i: (jnp.maximum((row_off // POOL_HALO) + (b * nt + i) * cp - 1, 0), 0)
    else:
        assert nt == 1
        ah_map = lambda b, i: (0, 0)
        ph_map = lambda b, i: (0, 0)
    vec = lambda n: pl.BlockSpec((1, n), lambda b, i: (0, 0))
    rc = min(t, 64)
    return pl.pallas_call(
        functools.partial(_convpool_body, t=t, rc=rc, start_pos=start_pos),
        grid=(batch, nt),
        in_specs=[
            pl.BlockSpec((t, D_CONV), lambda b, i: (off_t + b * nt + i, 0)),
            pl.BlockSpec((CONV_HALO, D_CONV), ah_map),
            pl.BlockSpec((1, CONV_HALO, D_CONV), lambda b, i: (b, 0, 0)),
            pl.BlockSpec((t, D_POOL), lambda b, i: (off_t + b * nt + i, 0)),
            pl.BlockSpec((POOL_HALO, D_POOL), ph_map),
            pl.BlockSpec((1, POOL_HALO, D_POOL), lambda b, i: (b, 0, 0)),
            pl.BlockSpec((CONV_HALO, D_CONV), lambda b, i: (0, 0)),
            vec(D_CONV), vec(D_CONV), vec(D_CONV),
            pl.BlockSpec((None, len(POOL_WINDOWS), POOL_GROUP, POOL_GROUP), lambda b, i: (e, 0, 0, 0)),
            vec(D_POOL),
        ],
        out_specs=pl.BlockSpec((t, D_CONV + D_POOL), lambda b, i: (b * nt + i, 0)),
        out_shape=jax.ShapeDtypeStruct((batch * seq, D_CONV + D_POOL), BF16),
        scratch_shapes=[
            pltpu.VMEM((CONV_HALO + t + SUBLANES, D_CONV), F32),
            pltpu.VMEM((POOL_HALO + t, D_POOL), F32),
            pltpu.VMEM((t, D_CONV), F32),
        ],
        compiler_params=_params("parallel", "arbitrary"),
        name="convpool",
    )(a, a, a_hist, p, p, p_hist, cw, cb, lng, lnb, pw, ps)


def _hgrn_body(q_ref, f_ref, v_ref, ga_ref, gn_ref, s0_ref, o_ref, sout_ref,
               st_ref, g_ref, m_ref, tril_ref, *, c, nc):
    ci = pl.program_id(1)
    levels = [1 << b for b in range(c.bit_length() - 1)]

    @pl.when(ci == 0)
    def _():
        for h in range(HEADS):
            st_ref[h] = s0_ref[0, h].T
        ri = lax.broadcasted_iota(jnp.int32, (c, c), 0)
        cj = lax.broadcasted_iota(jnp.int32, (c, c), 1)
        tril_ref[...] = (ri >= cj).astype(BF16)
        split = jnp.where(ri > cj, ri ^ cj, 0)
        for li in range(len(levels)):
            m_ref[li] = ((split >> li) == 1).astype(BF16)

    n = c // SUBLANES
    sub = lax.broadcasted_iota(jnp.int32, (1, SUBLANES, DK), 1)
    tiles = lambda arr: arr.reshape(n, SUBLANES, DK)

    def cumulate(h):
        hs = slice(h * DK, (h + 1) * DK)
        lf = jnp.log(f_ref[:, hs]) * LOG2_E
        hi = lf.astype(BF16)
        rest = lf - hi.astype(F32)
        mid = rest.astype(BF16)
        lo = (rest - mid.astype(F32)).astype(BF16)
        sums = jnp.dot(tril_ref[...], jnp.concatenate([hi, mid, lo], axis=1), preferred_element_type=F32)
        g_ref[:, hs] = sums[:, 0:DK] + sums[:, DK:2 * DK] + sums[:, 2 * DK:3 * DK]

    def attend(h):
        hs = slice(h * DK, (h + 1) * DK)
        q = q_ref[:, hs].astype(F32)
        f = f_ref[:, hs]
        k = 1.0 - f
        vb = v_ref[:, hs]
        v = vb.astype(F32)
        g = g_ref[:, hs]
        q3, k3, f3, g3 = tiles(q), tiles(k), tiles(f), tiles(g)
        qf3 = q3 * f3

        xs = []
        for s in levels:
            if s == 1:
                x = jnp.where((sub & 1) == 1, qf3, k3)
            elif s == 2:
                f_prev = pltpu.roll(f3, 1, 1)
                f_next = pltpu.roll(f3, SUBLANES - 1, 1)
                r4 = sub & 3
                x = jnp.where(r4 == 0, k3 * f_next,
                              jnp.where(r4 == 1, k3, jnp.where(r4 == 2, qf3, qf3 * f_prev)))
            elif s == 4:
                ref = jnp.broadcast_to(g3[:, 3:4, :], g3.shape)
                upper = (sub & 4) != 0
                x = jnp.where(upper, q3, k3) * jnp.exp2(jnp.where(upper, g3 - ref, ref - g3))
            else:
                pieces = []
                for pair in range(c // (2 * s)):
                    b0 = pair * 2 * s
                    ref = g_ref[b0 + s - 1:b0 + s, hs]
                    pieces.append(k[b0:b0 + s] * jnp.exp2(ref - g[b0:b0 + s]))
                    pieces.append(q[b0 + s:b0 + 2 * s] * jnp.exp2(g[b0 + s:b0 + 2 * s] - ref))
                x = jnp.concatenate(pieces, axis=0)
            xs.append(x.reshape(c, DK).astype(BF16))

        att = None
        for li, xb in enumerate(xs):
            term = lax.dot_general(xb, xb, NT_DIMS, preferred_element_type=F32).astype(BF16) * m_ref[li]
            att = term if att is None else att + term
        return q, k, v, vb, g, att

    def finish(h, q, k, v, vb, g, att):
        hs = slice(h * DK, (h + 1) * DK)
        o = jnp.dot(att, vb, preferred_element_type=F32)
        o = o + jnp.sum(q * k, axis=-1, keepdims=True) * v

        st = st_ref[h]
        qd = (q * jnp.exp2(g)).astype(BF16)
        o = o + lax.dot_general(qd, st.astype(BF16), NT_DIMS, preferred_element_type=F32)

        g_last = g_ref[c - 1:c, hs]
        kd = (k * jnp.exp2(g_last - g)).astype(BF16)
        st_ref[h] = st * jnp.exp2(g_last) + lax.dot_general(vb, kd, TN_DIMS, preferred_element_type=F32)

        on = o * lax.rsqrt(jnp.mean(o * o, axis=-1, keepdims=True) + EPS) * gn_ref[...]
        o_ref[:, hs] = (on * ga_ref[:, hs].astype(F32)).astype(BF16)

    pending = {}
    for step in range(HEADS + 2 * HEAD_SKEW):
        if step < HEADS:
            cumulate(step)
        if 0 <= step - HEAD_SKEW < HEADS:
            pending[step - HEAD_SKEW] = attend(step - HEAD_SKEW)
        if 0 <= step - 2 * HEAD_SKEW < HEADS:
            finish(step - 2 * HEAD_SKEW, *pending.pop(step - 2 * HEAD_SKEW))

    @pl.when(ci == nc - 1)
    def _():
        for h in range(HEADS):
            sout_ref[0, h] = st_ref[h].T


def _hgrn(q, f, v, ga, gnorm, s0, *, batch, seq, row_off, c):
    nc = seq // c
    off = row_off // c
    hd = HEADS * DK
    rows = pl.BlockSpec((c, hd), lambda b, i: (off + b * nc + i, 0))
    state = pl.BlockSpec((1, HEADS, DK, DV), lambda b, i: (b, 0, 0, 0))
    n_levels = c.bit_length() - 1
    return pl.pallas_call(
        functools.partial(_hgrn_body, c=c, nc=nc),
        grid=(batch, nc),
        in_specs=[rows, rows, rows, rows, pl.BlockSpec((1, DV), lambda b, i: (0, 0)), state],
        out_specs=[pl.BlockSpec((c, hd), lambda b, i: (b * nc + i, 0)), state],
        out_shape=[
            jax.ShapeDtypeStruct((batch * seq, hd), BF16),
            jax.ShapeDtypeStruct((batch, HEADS, DK, DV), F32),
        ],
        scratch_shapes=[
            pltpu.VMEM((HEADS, DV, DK), F32),
            pltpu.VMEM((c, hd), F32),
            pltpu.VMEM((n_levels, c, c), BF16),
            pltpu.VMEM((c, c), BF16),
        ],
        compiler_params=_params("parallel", "arbitrary"),
        name="hgrn",
    )(q, f, v, ga, gnorm, s0)


def _pad_hist(h, rows):
    b, n, d = h.shape
    return jnp.concatenate([jnp.zeros((b, rows - n, d), h.dtype), h], axis=1)


def _new_cache(hist, rows, batch, seq, off, keep):
    width = rows.shape[1]
    if seq >= keep:
        return rows[off:off + batch * seq].reshape(batch, seq, width)[:, seq - keep:]
    cur = rows[off:off + batch * seq].reshape(batch, seq, width)
    return jnp.concatenate([hist[:, seq:], cur], axis=1)


def kernel(x_prompt, x_sample, cache_conv, cache_pool, state_hgrn, ab_w_in, ab_w_out, conv_w, conv_b,
           conv_ln_g, conv_ln_b, pool_w, pool_scale, hgrn_w_in, hgrn_w_out, hgrn_gnorm, hgrn_lb, ffn_w_in,
           ffn_w_out, norm_g):
    bp, lp, d = x_prompt.shape
    bs, ls, _ = x_sample.shape
    mp, ms = bp * lp, bs * ls
    depth = ffn_w_in.shape[0]
    row = lambda vct: vct.reshape(1, -1).astype(F32)

    lb_all = jnp.cumsum(jax.nn.softmax(hgrn_lb.astype(F32), axis=0), axis=0)
    lb_all = lb_all - lb_all[0:1]

    ffn_w_in_b, ffn_w_out_b = ffn_w_in.astype(BF16), ffn_w_out.astype(BF16)
    ab_w_in_b, ab_w_out_b = ab_w_in.astype(BF16), ab_w_out.astype(BF16)
    hgrn_w_in_b, hgrn_w_out_b = hgrn_w_in.astype(BF16), hgrn_w_out.astype(BF16)
    pool_w_b = pool_w.astype(BF16)
    ffn = functools.partial(_ffn, w_in=ffn_w_in_b, w_out=ffn_w_out_b)

    conv_p, pool_p, hgrn_p, conv_s, pool_s, hgrn_s = [], [], [], [], [], []
    x = None
    for l in range(depth):
        g = norm_g[l]
        if l == 0:
            x = ffn(x_prompt.reshape(mp, d), row(g[0]), row(0.5 * g[1]), widx=(l, 0),
                    x_tail=x_sample.reshape(ms, d))
        else:
            x = ffn(x, row(g[0]), row(0.5 * g[1]), widx=(l, 0))
        if l % 2 == 0:
            e = l // 2
            a, p = _proj_glu(x, row(g[2]), ab_w_in_b, e)
            cw = jnp.concatenate([conv_w[e], jnp.zeros((CONV_HALO - CONV_WIDTH, D_CONV), F32)], axis=0)
            shared = (cw, row(conv_b[e]), row(conv_ln_g[e]), row(conv_ln_b[e]), pool_w_b, row(pool_scale[e]))
            hist_c, hist_p = cache_conv[e].astype(F32), cache_pool[e].astype(F32)
            zero_c = jnp.zeros((bp, CONV_WIDTH - 1, D_CONV), F32)
            zero_p = jnp.zeros((bp, POOL_MAX - 1, D_POOL), F32)
            y_p = _convpool(a, p, _pad_hist(zero_c, CONV_HALO), _pad_hist(zero_p, POOL_HALO), *shared,
                            e=e, batch=bp, seq=lp, row_off=0, t=512, start_pos=0)
            y_s = _convpool(a, p, _pad_hist(hist_c, CONV_HALO), _pad_hist(hist_p, POOL_HALO), *shared,
                            e=e, batch=bs, seq=ls, row_off=mp, t=ls, start_pos=PAST_LEN)
            w_out, wi = ab_w_out_b, e
            conv_p.append(_new_cache(zero_c, a, bp, lp, 0, CONV_WIDTH - 1))
            pool_p.append(_new_cache(zero_p, p, bp, lp, 0, POOL_MAX - 1))
            conv_s.append(_new_cache(hist_c, a, bs, ls, mp, CONV_WIDTH - 1))
            pool_s.append(_new_cache(hist_p, p, bs, ls, mp, POOL_MAX - 1))
        else:
            o = l // 2
            q, f, v, ga = _proj_hgrn(x, row(g[2]), row(lb_all[l]), hgrn_w_in_b, o)
            gn = row(hgrn_gnorm[o])
            zero_s = jnp.zeros((bp, HEADS, DK, DV), F32)
            y_p, s_p = _hgrn(q, f, v, ga, gn, zero_s, batch=bp, seq=lp, row_off=0, c=128)
            y_s, s_s = _hgrn(q, f, v, ga, gn, state_hgrn[o].astype(F32), batch=bs, seq=ls, row_off=mp, c=ls)
            w_out, wi = hgrn_w_out_b, o
            hgrn_p.append(s_p)
            hgrn_s.append(s_s.astype(state_hgrn.dtype))
        x = _proj_out(y_p, y_s, w_out, wi, x, row(g[3]))
        if l == depth - 1:
            x_p, x_s = ffn(x, row(g[4]), row(0.5 * g[5]), widx=(l, 1), split_rows=mp)
        else:
            x = ffn(x, row(g[4]), row(0.5 * g[5]), widx=(l, 1))

    return (x_p.reshape(bp, lp, d), x_s.reshape(bs, ls, d),
            jnp.stack(conv_p), jnp.stack(pool_p), jnp.stack(hgrn_p),
            jnp.stack(conv_s), jnp.stack(pool_s), jnp.stack(hgrn_s))
```

```python
import functools

import jax
import jax.numpy as jnp
from jax import lax
from jax.experimental import pallas as pl
from jax.experimental.pallas import tpu as pltpu

F32 = jnp.float32
BF16 = jnp.bfloat16

D_MODEL = 2048
D_CONV = 1024
D_POOL = 1024
CONV_WIDTH = 31
POOL_WINDOWS = (2, 4, 8, 16)
POOL_GROUP = 256
POOL_MAX = 16
HEADS = 16
DK = 128
DV = 128
D_FF = 5632
EPS = 1e-6
PAST_LEN = 4096
LOG2_E = 1.4426950408889634

VMEM_LIMIT_BYTES = 54 * 1024 * 1024
ROW_TILE = 512
SUBLANES = 8
CAST_ROWS = 16
CONV_HALO = 32
POOL_HALO = 16
HEAD_SKEW = 2
NT_DIMS = (((1,), (1,)), ((), ()))
TN_DIMS = (((0,), (0,)), ((), ()))


def _rms(x, g):
    return x * lax.rsqrt(jnp.mean(x * x, axis=-1, keepdims=True) + EPS) * g


def _sigmoid(x):
    return 1.0 / (1.0 + jnp.exp(-x))


def _params(*sem):
    return pltpu.CompilerParams(dimension_semantics=sem, vmem_limit_bytes=VMEM_LIMIT_BYTES)


def _head_tail_specs(shape, n_head):
    head = pl.BlockSpec(shape, lambda i, *_: (jnp.minimum(i, n_head - 1), 0))
    tail = pl.BlockSpec(shape, lambda i, *_: (jnp.maximum(i - n_head, 0), 0))
    return head, tail


def _by_trunk(i, n_head, head_refs, tail_refs, fn):
    @pl.when(i < n_head)
    def _():
        fn(*head_refs)

    @pl.when(i >= n_head)
    def _():
        fn(*tail_refs)


def _ffn_body(*refs, nf, n_head, pair_in, pair_out, n_cast):
    refs = list(refs)
    x_refs = [refs.pop(0) for _ in range(2 if pair_in else 1)]
    g0_ref, g1_ref, wa_ref, wb_ref, wo_ref = (refs.pop(0) for _ in range(5))
    cast_in = [refs.pop(0) for _ in range(n_cast)]
    o_refs = [refs.pop(0) for _ in range(2 if pair_out else 1)]
    cast_out = [refs.pop(0) for _ in range(n_cast)]
    xn_ref, acc_ref = refs
    i, f = pl.program_id(0), pl.program_id(1)

    for src_ref, dst_ref in zip(cast_in, cast_out):
        dst_ref[...] = src_ref[...].astype(BF16)

    def by_trunk(fn):
        if pair_in or pair_out:
            _by_trunk(i, n_head, (x_refs[0], o_refs[0]), (x_refs[-1], o_refs[-1]), fn)
        else:
            fn(x_refs[0], o_refs[0])

    @pl.when(f == 0)
    def _():
        def prologue(x_ref, _):
            xn_ref[...] = _rms(x_ref[...], g0_ref[...]).astype(BF16)

        by_trunk(prologue)
        acc_ref[...] = jnp.zeros(acc_ref.shape, F32)

    xn = xn_ref[...]
    a = jnp.dot(xn, wa_ref[...], preferred_element_type=F32)
    b = jnp.dot(xn, wb_ref[...], preferred_element_type=F32)
    h = (a * _sigmoid(a) * b).astype(BF16)
    acc_ref[...] += jnp.dot(h, wo_ref[...], preferred_element_type=F32)

    @pl.when(f == nf - 1)
    def _():
        def epilogue(x_ref, o_ref):
            o_ref[...] = x_ref[...] + _rms(acc_ref[...], g1_ref[...])

        by_trunk(epilogue)


def _ffn(x, g_pre, g_post, w_in, w_out, *, x_tail=None, split_rows=None, cast=(), tm=ROW_TILE, tf=512):
    d = x.shape[1]
    m = x.shape[0] + (0 if x_tail is None else x_tail.shape[0])
    nf = D_FF // tf
    n_steps = (m // tm) * nf
    n_head = (x.shape[0] if x_tail is not None else (split_rows or m)) // tm
    row_spec = pl.BlockSpec((tm, d), lambda i, f: (i, 0))
    vec_spec = pl.BlockSpec((1, d), lambda i, f: (0, 0))
    x_specs = list(_head_tail_specs((tm, d), n_head)) if x_tail is not None else [row_spec]
    x_args = [x, x_tail] if x_tail is not None else [x]
    if split_rows is not None:
        out_specs = list(_head_tail_specs((tm, d), n_head))
        out_shape = [jax.ShapeDtypeStruct((split_rows, d), F32), jax.ShapeDtypeStruct((m - split_rows, d), F32)]
    else:
        out_specs, out_shape = [row_spec], [jax.ShapeDtypeStruct((m, d), F32)]

    cast_in_specs, cast_out_specs, cast_shapes = [], [], []
    for arr, lead in cast:
        rows, cols = arr.shape[-2:]
        n_slabs = rows // CAST_ROWS
        assert rows % CAST_ROWS == 0 and n_slabs <= n_steps and len(lead) == arr.ndim - 2
        slab = lambda i, f, n_slabs=n_slabs: jnp.minimum(i * nf + f, n_slabs - 1)
        cast_in_specs.append(pl.BlockSpec((None,) * len(lead) + (CAST_ROWS, cols),
                                          lambda i, f, lead=lead, slab=slab: (*lead, slab(i, f), 0)))
        cast_out_specs.append(pl.BlockSpec((CAST_ROWS, cols), lambda i, f, slab=slab: (slab(i, f), 0)))
        cast_shapes.append(jax.ShapeDtypeStruct((rows, cols), BF16))

    in_order = split_rows is not None or bool(cast)
    outs = pl.pallas_call(
        functools.partial(_ffn_body, nf=nf, n_head=n_head, pair_in=x_tail is not None,
                          pair_out=split_rows is not None, n_cast=len(cast)),
        grid=(m // tm, nf),
        in_specs=x_specs + [
            vec_spec, vec_spec,
            pl.BlockSpec((d, tf), lambda i, f: (0, f)),
            pl.BlockSpec((d, tf), lambda i, f: (0, f + nf)),
            pl.BlockSpec((tf, d), lambda i, f: (f, 0)),
        ] + cast_in_specs,
        out_specs=out_specs + cast_out_specs,
        out_shape=out_shape + cast_shapes,
        scratch_shapes=[pltpu.VMEM((tm, d), BF16), pltpu.VMEM((tm, d), F32)],
        compiler_params=_params("arbitrary" if in_order else "parallel", "arbitrary"),
        name="ffn",
    )(*x_args, g_pre, g_post, w_in, w_in, w_out, *[arr for arr, _ in cast])
    n_main = len(out_shape)
    main = outs[0] if n_main == 1 else tuple(outs[:n_main])
    return main, list(outs[n_main:])


def _proj_glu_body(x_ref, g_ref, wv_ref, wg_ref, wp_ref, a_ref, p_ref):
    xn = _rms(x_ref[...], g_ref[...]).astype(BF16)
    val = jnp.dot(xn, wv_ref[...], preferred_element_type=F32)
    gate = jnp.dot(xn, wg_ref[...], preferred_element_type=F32)
    a_ref[...] = val * _sigmoid(gate)
    p_ref[...] = jnp.dot(xn, wp_ref[...], preferred_element_type=F32)


def _proj_glu(x, g, w, *, tm=ROW_TILE):
    m, d = x.shape
    wspec = lambda part: pl.BlockSpec((d, D_CONV), lambda i: (0, part), pipeline_mode=pl.Buffered(1))
    return pl.pallas_call(
        _proj_glu_body,
        grid=(m // tm,),
        in_specs=[
            pl.BlockSpec((tm, d), lambda i: (i, 0)),
            pl.BlockSpec((1, d), lambda i: (0, 0)),
            wspec(0), wspec(1), wspec(2),
        ],
        out_specs=[
            pl.BlockSpec((tm, D_CONV), lambda i: (i, 0)),
            pl.BlockSpec((tm, D_POOL), lambda i: (i, 0)),
        ],
        out_shape=[jax.ShapeDtypeStruct((m, D_CONV), F32), jax.ShapeDtypeStruct((m, D_POOL), F32)],
        compiler_params=_params("parallel"),
        name="proj_glu",
    )(x, g, w, w, w)


def _proj_hgrn_body(x_ref, g_ref, lb_ref, wq_ref, wf_ref, wv_ref, wg_ref, q_ref, f_ref, v_ref, ga_ref, xn_ref):
    @pl.when(pl.program_id(1) == 0)
    def _():
        xn_ref[...] = _rms(x_ref[...], g_ref[...]).astype(BF16)

    xn = xn_ref[...]
    q = jnp.dot(xn, wq_ref[...], preferred_element_type=F32)
    q_ref[...] = (q * _sigmoid(q) * (DK ** -0.5)).astype(BF16)
    fz = jnp.dot(xn, wf_ref[...], preferred_element_type=F32)
    lb = lb_ref[...]
    f_ref[...] = lb + (1.0 - lb) * _sigmoid(fz)
    v_ref[...] = jnp.dot(xn, wv_ref[...], preferred_element_type=F32).astype(BF16)
    gz = jnp.dot(xn, wg_ref[...], preferred_element_type=F32)
    ga_ref[...] = (gz * _sigmoid(gz)).astype(BF16)


def _proj_hgrn(x, g, lb, w, *, tm=ROW_TILE, tn=512):
    m, d = x.shape
    hd = HEADS * DK
    nj = hd // tn
    wspec = lambda part: pl.BlockSpec((d, tn), lambda i, j: (0, j + part * nj))
    ospec = pl.BlockSpec((tm, tn), lambda i, j: (i, j))
    act = jax.ShapeDtypeStruct((m, hd), BF16)
    return pl.pallas_call(
        _proj_hgrn_body,
        grid=(m // tm, nj),
        in_specs=[
            pl.BlockSpec((tm, d), lambda i, j: (i, 0)),
            pl.BlockSpec((1, d), lambda i, j: (0, 0)),
            pl.BlockSpec((1, tn), lambda i, j: (0, j)),
            wspec(0), wspec(1), wspec(2), wspec(3),
        ],
        out_specs=[ospec, ospec, ospec, ospec],
        out_shape=[act, jax.ShapeDtypeStruct((m, hd), F32), act, act],
        scratch_shapes=[pltpu.VMEM((tm, d), BF16)],
        compiler_params=_params("parallel", "arbitrary"),
        name="proj_hgrn",
    )(x, g, lb, w, w, w, w)


def _proj_out_body(yh_ref, yt_ref, w_ref, x_ref, g_ref, o_ref, *, n_head):
    def project(y_ref):
        half = y_ref.shape[0] // 2
        for r in (slice(0, half), slice(half, 2 * half)):
            m = jnp.dot(y_ref[r, :], w_ref[...], preferred_element_type=F32)
            o_ref[r, :] = x_ref[r, :] + _rms(m, g_ref[...])

    _by_trunk(pl.program_id(0), n_head, (yh_ref,), (yt_ref,), project)


def _proj_out(y_head, y_tail, w, x, g, *, tm=ROW_TILE):
    m, d = x.shape
    kdim = w.shape[0]
    n_head = y_head.shape[0] // tm
    return pl.pallas_call(
        functools.partial(_proj_out_body, n_head=n_head),
        grid=(m // tm,),
        in_specs=list(_head_tail_specs((tm, kdim), n_head)) + [
            pl.BlockSpec((kdim, d), lambda i: (0, 0)),
            pl.BlockSpec((tm, d), lambda i: (i, 0)),
            pl.BlockSpec((1, d), lambda i: (0, 0)),
        ],
        out_specs=pl.BlockSpec((tm, d), lambda i: (i, 0)),
        out_shape=jax.ShapeDtypeStruct((m, d), F32),
        compiler_params=_params("parallel"),
        name="proj_out",
    )(y_head, y_tail, w, x, g)


def _convpool_body(a_ref, ah_ref, ahist_ref, p_ref, ph_ref, phist_ref, cw_ref, cb_ref, lng_ref,
                   lnb_ref, pw_ref, ps_ref, o_ref, aext_ref, pext_ref, y_ref, *, t, rc, start_pos):
    i = pl.program_id(1)

    @pl.when(i == 0)
    def _():
        aext_ref[0:CONV_HALO, :] = ahist_ref[0]
        pext_ref[0:POOL_HALO, :] = phist_ref[0]

    @pl.when(i > 0)
    def _():
        aext_ref[0:CONV_HALO, :] = ah_ref[...]
        pext_ref[0:POOL_HALO, :] = ph_ref[...]

    aext_ref[CONV_HALO:CONV_HALO + t, :] = a_ref[...]
    aext_ref[CONV_HALO + t:CONV_HALO + t + SUBLANES, :] = jnp.zeros((SUBLANES, D_CONV), F32)
    pext_ref[POOL_HALO:POOL_HALO + t, :] = p_ref[...]

    lead = CONV_HALO - (CONV_WIDTH - 1)

    def conv_rows(r, carry):
        r0 = pl.multiple_of(r * rc, rc)
        for c in range(D_CONV // 128):
            cs = slice(c * 128, (c + 1) * 128)
            win = aext_ref[pl.ds(r0, rc + CONV_HALO + SUBLANES), cs]
            acc = None
            for ph in range(SUBLANES):
                part = None
                for q in range(CONV_HALO // SUBLANES + 1):
                    w = SUBLANES * q + ph - lead
                    if 0 <= w < CONV_WIDTH:
                        term = win[SUBLANES * q:SUBLANES * q + rc + SUBLANES] * cw_ref[w:w + 1, cs]
                        part = term if part is None else part + term
                shifted = part[ph:ph + rc]
                acc = shifted if acc is None else acc + shifted
            y_ref[pl.ds(r0, rc), cs] = acc + cb_ref[:, cs]
        return carry

    lax.fori_loop(0, t // rc, conv_rows, 0)

    y = y_ref[...]
    mu = jnp.mean(y, axis=-1, keepdims=True)
    yc = y - mu
    z = yc * lax.rsqrt(jnp.mean(yc * yc, axis=-1, keepdims=True) + EPS) * lng_ref[...] + lnb_ref[...]
    o_ref[:, 0:D_CONV] = (z * _sigmoid(z)).astype(BF16)

    pos = start_pos + i * t + lax.broadcasted_iota(jnp.int32, (t, 1), 0)
    for gi, win_len in enumerate(POOL_WINDOWS):
        cs = slice(gi * POOL_GROUP, (gi + 1) * POOL_GROUP)
        cur = pext_ref[POOL_HALO:POOL_HALO + t, cs]
        wsum = cur
        for dlt in range(1, win_len):
            wsum = wsum + pext_ref[POOL_HALO - dlt:POOL_HALO - dlt + t, cs]
        cnt = jnp.minimum(win_len, pos + 1).astype(F32)
        dev = (wsum / cnt - cur).astype(BF16)
        mixed = jnp.dot(dev, pw_ref[gi], preferred_element_type=F32) * ps_ref[:, cs]
        o_ref[:, D_CONV + gi * POOL_GROUP:D_CONV + (gi + 1) * POOL_GROUP] = mixed.astype(BF16)


def _convpool(a, p, a_hist, p_hist, cw, cb, lng, lnb, pw, ps, *, e, batch, seq, row_off, t, start_pos):
    nt = seq // t
    off_t, ca, cp = row_off // t, t // CONV_HALO, t // POOL_HALO
    if t >= CONV_HALO:
        ah_map = lambda b, i: (jnp.maximum((row_off // CONV_HALO) + (b * nt + i) * ca - 1, 0), 0)
        ph_map = lambda b, i: (jnp.maximum((row_off // POOL_HALO) + (b * nt + i) * cp - 1, 0), 0)
    else:
        assert nt == 1
        ah_map = lambda b, i: (0, 0)
        ph_map = lambda b, i: (0, 0)
    vec = lambda n: pl.BlockSpec((1, n), lambda b, i: (0, 0))
    rc = min(t, 64)
    return pl.pallas_call(
        functools.partial(_convpool_body, t=t, rc=rc, start_pos=start_pos),
        grid=(batch, nt),
        in_specs=[
            pl.BlockSpec((t, D_CONV), lambda b, i: (off_t + b * nt + i, 0)),
            pl.BlockSpec((CONV_HALO, D_CONV), ah_map),
            pl.BlockSpec((1, CONV_HALO, D_CONV), lambda b, i: (b, 0, 0)),
            pl.BlockSpec((t, D_POOL), lambda b, i: (off_t + b * nt + i, 0)),
            pl.BlockSpec((POOL_HALO, D_POOL), ph_map),
            pl.BlockSpec((1, POOL_HALO, D_POOL), lambda b, i: (b, 0, 0)),
            pl.BlockSpec((CONV_HALO, D_CONV), lambda b, i: (0, 0)),
            vec(D_CONV), vec(D_CONV), vec(D_CONV),
            pl.BlockSpec((None, len(POOL_WINDOWS), POOL_GROUP, POOL_GROUP), lambda b, i: (e, 0, 0, 0)),
            vec(D_POOL),
        ],
        out_specs=pl.BlockSpec((t, D_CONV + D_POOL), lambda b, i: (b * nt + i, 0)),
        out_shape=jax.ShapeDtypeStruct((batch * seq, D_CONV + D_POOL), BF16),
        scratch_shapes=[
            pltpu.VMEM((CONV_HALO + t + SUBLANES, D_CONV), F32),
            pltpu.VMEM((POOL_HALO + t, D_POOL), F32),
            pltpu.VMEM((t, D_CONV), F32),
        ],
        compiler_params=_params("parallel", "arbitrary"),
        name="convpool",
    )(a, a, a_hist, p, p, p_hist, cw, cb, lng, lnb, pw, ps)


def _hgrn_body(q_ref, f_ref, v_ref, ga_ref, gn_ref, s0_ref, o_ref, sout_ref,
               st_ref, g_ref, m_ref, tril_ref, *, c, nc):
    ci = pl.program_id(1)
    levels = [1 << b for b in range(c.bit_length() - 1)]

    @pl.when(ci == 0)
    def _():
        for h in range(HEADS):
            st_ref[h] = s0_ref[0, h].T
        ri = lax.broadcasted_iota(jnp.int32, (c, c), 0)
        cj = lax.broadcasted_iota(jnp.int32, (c, c), 1)
        tril_ref[...] = (ri >= cj).astype(BF16)
        split = jnp.where(ri > cj, ri ^ cj, 0)
        for li in range(len(levels)):
            m_ref[li] = ((split >> li) == 1).astype(BF16)

    n = c // SUBLANES
    sub = lax.broadcasted_iota(jnp.int32, (1, SUBLANES, DK), 1)
    tiles = lambda arr: arr.reshape(n, SUBLANES, DK)

    def cumulate(h):
        hs = slice(h * DK, (h + 1) * DK)
        lf = jnp.log(f_ref[:, hs]) * LOG2_E
        hi = lf.astype(BF16)
        rest = lf - hi.astype(F32)
        mid = rest.astype(BF16)
        lo = (rest - mid.astype(F32)).astype(BF16)
        sums = jnp.dot(tril_ref[...], jnp.concatenate([hi, mid, lo], axis=1), preferred_element_type=F32)
        g_ref[:, hs] = sums[:, 0:DK] + sums[:, DK:2 * DK] + sums[:, 2 * DK:3 * DK]

    def attend(h):
        hs = slice(h * DK, (h + 1) * DK)
        q = q_ref[:, hs].astype(F32)
        f = f_ref[:, hs]
        k = 1.0 - f
        vb = v_ref[:, hs]
        v = vb.astype(F32)
        g = g_ref[:, hs]
        q3, k3, f3, g3 = tiles(q), tiles(k), tiles(f), tiles(g)
        qf3 = q3 * f3

        xs = []
        for s in levels:
            if s == 1:
                x = jnp.where((sub & 1) == 1, qf3, k3)
            elif s == 2:
                f_prev = pltpu.roll(f3, 1, 1)
                f_next = pltpu.roll(f3, SUBLANES - 1, 1)
                r4 = sub & 3
                x = jnp.where(r4 == 0, k3 * f_next,
                              jnp.where(r4 == 1, k3, jnp.where(r4 == 2, qf3, qf3 * f_prev)))
            elif s == 4:
                ref = jnp.broadcast_to(g3[:, 3:4, :], g3.shape)
                upper = (sub & 4) != 0
                x = jnp.where(upper, q3, k3) * jnp.exp2(jnp.where(upper, g3 - ref, ref - g3))
            else:
                pieces = []
                for pair in range(c // (2 * s)):
                    b0 = pair * 2 * s
                    ref = g_ref[b0 + s - 1:b0 + s, hs]
                    pieces.append(k[b0:b0 + s] * jnp.exp2(ref - g[b0:b0 + s]))
                    pieces.append(q[b0 + s:b0 + 2 * s] * jnp.exp2(g[b0 + s:b0 + 2 * s] - ref))
                x = jnp.concatenate(pieces, axis=0)
            xs.append(x.reshape(c, DK).astype(BF16))

        att = None
        for li, xb in enumerate(xs):
            term = lax.dot_general(xb, xb, NT_DIMS, preferred_element_type=F32).astype(BF16) * m_ref[li]
            att = term if att is None else att + term
        return q, k, v, vb, g, att

    def finish(h, q, k, v, vb, g, att):
        hs = slice(h * DK, (h + 1) * DK)
        o = jnp.dot(att, vb, preferred_element_type=F32)
        o = o + jnp.sum(q * k, axis=-1, keepdims=True) * v

        st = st_ref[h]
        qd = (q * jnp.exp2(g)).astype(BF16)
        o = o + lax.dot_general(qd, st.astype(BF16), NT_DIMS, preferred_element_type=F32)

        g_last = g_ref[c - 1:c, hs]
        kd = (k * jnp.exp2(g_last - g)).astype(BF16)
        st_ref[h] = st * jnp.exp2(g_last) + lax.dot_general(vb, kd, TN_DIMS, preferred_element_type=F32)

        on = o * lax.rsqrt(jnp.mean(o * o, axis=-1, keepdims=True) + EPS) * gn_ref[...]
        o_ref[:, hs] = (on * ga_ref[:, hs].astype(F32)).astype(BF16)

    pending = {}
    for step in range(HEADS + 2 * HEAD_SKEW):
        if step < HEADS:
            cumulate(step)
        if 0 <= step - HEAD_SKEW < HEADS:
            pending[step - HEAD_SKEW] = attend(step - HEAD_SKEW)
        if 0 <= step - 2 * HEAD_SKEW < HEADS:
            finish(step - 2 * HEAD_SKEW, *pending.pop(step - 2 * HEAD_SKEW))

    @pl.when(ci == nc - 1)
    def _():
        for h in range(HEADS):
            sout_ref[0, h] = st_ref[h].T


def _hgrn(q, f, v, ga, gnorm, s0, *, batch, seq, row_off, c):
    nc = seq // c
    off = row_off // c
    hd = HEADS * DK
    rows = pl.BlockSpec((c, hd), lambda b, i: (off + b * nc + i, 0))
    state = pl.BlockSpec((1, HEADS, DK, DV), lambda b, i: (b, 0, 0, 0))
    n_levels = c.bit_length() - 1
    return pl.pallas_call(
        functools.partial(_hgrn_body, c=c, nc=nc),
        grid=(batch, nc),
        in_specs=[rows, rows, rows, rows, pl.BlockSpec((1, DV), lambda b, i: (0, 0)), state],
        out_specs=[pl.BlockSpec((c, hd), lambda b, i: (b * nc + i, 0)), state],
        out_shape=[
            jax.ShapeDtypeStruct((batch * seq, hd), BF16),
            jax.ShapeDtypeStruct((batch, HEADS, DK, DV), F32),
        ],
        scratch_shapes=[
            pltpu.VMEM((HEADS, DV, DK), F32),
            pltpu.VMEM((c, hd), F32),
            pltpu.VMEM((n_levels, c, c), BF16),
            pltpu.VMEM((c, c), BF16),
        ],
        compiler_params=_params("parallel", "arbitrary"),
        name="hgrn",
    )(q, f, v, ga, gnorm, s0)


def _pad_hist(h, rows):
    b, n, d = h.shape
    return jnp.concatenate([jnp.zeros((b, rows - n, d), h.dtype), h], axis=1)


def _new_cache(hist, rows, batch, seq, off, keep):
    if seq >= keep:
        return jnp.stack([rows[off + (b + 1) * seq - keep:off + (b + 1) * seq] for b in range(batch)])
    cur = rows[off:off + batch * seq].reshape(batch, seq, rows.shape[1])
    return jnp.concatenate([hist[:, seq:], cur], axis=1)


def kernel(x_prompt, x_sample, cache_conv, cache_pool, state_hgrn, ab_w_in, ab_w_out, conv_w, conv_b,
           conv_ln_g, conv_ln_b, pool_w, pool_scale, hgrn_w_in, hgrn_w_out, hgrn_gnorm, hgrn_lb, ffn_w_in,
           ffn_w_out, norm_g):
    bp, lp, d = x_prompt.shape
    bs, ls, _ = x_sample.shape
    mp, ms = bp * lp, bs * ls
    depth = ffn_w_in.shape[0]
    row = lambda vct: vct.reshape(1, -1).astype(F32)

    lb_all = jnp.cumsum(jax.nn.softmax(hgrn_lb.astype(F32), axis=0), axis=0)
    lb_all = lb_all - lb_all[0:1]

    pool_w_b = pool_w.astype(BF16)
    w_in_b, w_out_b = ffn_w_in[0, 0].astype(BF16), ffn_w_out[0, 0].astype(BF16)

    conv_p, pool_p, hgrn_p, conv_s, pool_s, hgrn_s = [], [], [], [], [], []
    x = None
    for l in range(depth):
        g = norm_g[l]
        mixer_w = (ab_w_in, ab_w_out) if l % 2 == 0 else (hgrn_w_in, hgrn_w_out)
        cast = [(ffn_w_in, (l, 1)), (ffn_w_out, (l, 1)), (mixer_w[0], (l // 2,)), (mixer_w[1], (l // 2,))]
        if l == 0:
            x, cast_w = _ffn(x_prompt.reshape(mp, d), row(g[0]), row(0.5 * g[1]), w_in_b, w_out_b,
                             x_tail=x_sample.reshape(ms, d), cast=cast)
        else:
            x, cast_w = _ffn(x, row(g[0]), row(0.5 * g[1]), w_in_b, w_out_b, cast=cast)
        w_in_b, w_out_b, mix_in_b, mix_out_b = cast_w
        if l % 2 == 0:
            e = l // 2
            a, p = _proj_glu(x, row(g[2]), mix_in_b)
            cw = jnp.concatenate([conv_w[e], jnp.zeros((CONV_HALO - CONV_WIDTH, D_CONV), F32)], axis=0)
            shared = (cw, row(conv_b[e]), row(conv_ln_g[e]), row(conv_ln_b[e]), pool_w_b, row(pool_scale[e]))
            hist_c, hist_p = cache_conv[e].astype(F32), cache_pool[e].astype(F32)
            zero_c = jnp.zeros((bp, CONV_WIDTH - 1, D_CONV), F32)
            zero_p = jnp.zeros((bp, POOL_MAX - 1, D_POOL), F32)
            y_p = _convpool(a, p, _pad_hist(zero_c, CONV_HALO), _pad_hist(zero_p, POOL_HALO), *shared,
                            e=e, batch=bp, seq=lp, row_off=0, t=512, start_pos=0)
            y_s = _convpool(a, p, _pad_hist(hist_c, CONV_HALO), _pad_hist(hist_p, POOL_HALO), *shared,
                            e=e, batch=bs, seq=ls, row_off=mp, t=ls, start_pos=PAST_LEN)
            conv_p.append(_new_cache(zero_c, a, bp, lp, 0, CONV_WIDTH - 1))
            pool_p.append(_new_cache(zero_p, p, bp, lp, 0, POOL_MAX - 1))
            conv_s.append(_new_cache(hist_c, a, bs, ls, mp, CONV_WIDTH - 1))
            pool_s.append(_new_cache(hist_p, p, bs, ls, mp, POOL_MAX - 1))
        else:
            o = l // 2
            q, f, v, ga = _proj_hgrn(x, row(g[2]), row(lb_all[l]), mix_in_b)
            gn = row(hgrn_gnorm[o])
            zero_s = jnp.zeros((bp, HEADS, DK, DV), F32)
            y_p, s_p = _hgrn(q, f, v, ga, gn, zero_s, batch=bp, seq=lp, row_off=0, c=128)
            y_s, s_s = _hgrn(q, f, v, ga, gn, state_hgrn[o].astype(F32), batch=bs, seq=ls, row_off=mp, c=ls)
            hgrn_p.append(s_p)
            hgrn_s.append(s_s.astype(state_hgrn.dtype))
        x = _proj_out(y_p, y_s, mix_out_b, x, row(g[3]))
        if l == depth - 1:
            (x_p, x_s), _ = _ffn(x, row(g[4]), row(0.5 * g[5]), w_in_b, w_out_b, split_rows=mp)
        else:
            x, (w_in_b, w_out_b) = _ffn(x, row(g[4]), row(0.5 * g[5]), w_in_b, w_out_b,
                                        cast=[(ffn_w_in, (l + 1, 0)), (ffn_w_out, (l + 1, 0))])

    return (x_p.reshape(bp, lp, d), x_s.reshape(bs, ls, d),
            jnp.stack(conv_p), jnp.stack(pool_p), jnp.stack(hgrn_p),
            jnp.stack(conv_s), jnp.stack(pool_s), jnp.stack(hgrn_s))
```

```python
import functools

import jax
import jax.numpy as jnp
from jax import lax
from jax.experimental import pallas as pl
from jax.experimental.pallas import tpu as pltpu

F32 = jnp.float32
BF16 = jnp.bfloat16

D_MODEL = 2048
D_CONV = 1024
D_POOL = 1024
CONV_WIDTH = 31
POOL_WINDOWS = (2, 4, 8, 16)
POOL_GROUP = 256
POOL_MAX = 16
HEADS = 16
DK = 128
DV = 128
D_FF = 5632
EPS = 1e-6
PAST_LEN = 4096
LOG2_E = 1.4426950408889634
LOG2_FLOOR = -150.0

VMEM_LIMIT_BYTES = 54 * 1024 * 1024
ROW_TILE = 512
WIDE_ROW_TILE = 768
SUBLANES = 8
NORM_ROWS = 32
CAST_ROWS = 16
CONV_HALO = 32
POOL_HALO = 16
HEAD_SKEW = 2
NT_DIMS = (((1,), (1,)), ((), ()))
TN_DIMS = (((0,), (0,)), ((), ()))


def _rms(x, g):
    return x * lax.rsqrt(jnp.mean(x * x, axis=-1, keepdims=True) + EPS) * g


def _sigmoid(x):
    return 1.0 / (1.0 + jnp.exp(-x))


def _params(*sem):
    return pltpu.CompilerParams(dimension_semantics=sem, vmem_limit_bytes=VMEM_LIMIT_BYTES)


def _head_tail_specs(shape, n_head):
    head = pl.BlockSpec(shape, lambda i, *_: (jnp.minimum(i, n_head - 1), 0))
    tail = pl.BlockSpec(shape, lambda i, *_: (jnp.maximum(i - n_head, 0), 0))
    return head, tail


def _row_chunks(n_rows, fn):
    for c in range(n_rows // NORM_ROWS):
        fn(c * NORM_ROWS)


def _by_trunk(i, n_head, head_refs, tail_refs, fn):
    @pl.when(i < n_head)
    def _():
        fn(*head_refs)

    @pl.when(i >= n_head)
    def _():
        fn(*tail_refs)


def _ffn_body(*refs, nf, n_head, pair_in, pair_out, n_cast):
    refs = list(refs)
    x_refs = [refs.pop(0) for _ in range(2 if pair_in else 1)]
    g0_ref, g1_ref, wa_ref, wb_ref, wo_ref = (refs.pop(0) for _ in range(5))
    cast_in = [refs.pop(0) for _ in range(n_cast)]
    o_refs = [refs.pop(0) for _ in range(2 if pair_out else 1)]
    cast_out = [refs.pop(0) for _ in range(n_cast)]
    xn_ref = refs.pop(0)
    acc_ref = refs.pop(0) if pair_out else o_refs[0]
    i, f = pl.program_id(0), pl.program_id(1)

    for src_ref, dst_ref in zip(cast_in, cast_out):
        dst_ref[...] = src_ref[...].astype(BF16)

    def by_trunk(fn):
        if pair_in or pair_out:
            _by_trunk(i, n_head, (x_refs[0], o_refs[0]), (x_refs[-1], o_refs[-1]), fn)
        else:
            fn(x_refs[0], o_refs[0])

    @pl.when(f == 0)
    def _():
        def prologue(x_ref, _):
            xn_ref[...] = _rms(x_ref[...], g0_ref[...]).astype(BF16)

        by_trunk(prologue)
        acc_ref[...] = jnp.zeros(acc_ref.shape, F32)

    xn = xn_ref[...]
    a = jnp.dot(xn, wa_ref[...], preferred_element_type=F32)
    b = jnp.dot(xn, wb_ref[...], preferred_element_type=F32)
    h = (a * _sigmoid(a) * b).astype(BF16)
    acc_ref[...] += jnp.dot(h, wo_ref[...], preferred_element_type=F32)

    @pl.when(f == nf - 1)
    def _():
        def epilogue(x_ref, o_ref):
            def rows(r0):
                r = pl.ds(r0, NORM_ROWS)
                o_ref[r, :] = x_ref[r, :] + _rms(acc_ref[r, :], g1_ref[...])

            _row_chunks(x_ref.shape[0], rows)

        by_trunk(epilogue)


def _ffn(x, g_pre, g_post, w_in, w_out, *, x_tail=None, split_rows=None, cast=(), tf=512):
    d = x.shape[1]
    m = x.shape[0] + (0 if x_tail is None else x_tail.shape[0])
    tm = ROW_TILE if (x_tail is not None or split_rows is not None) else WIDE_ROW_TILE
    nf = D_FF // tf
    n_steps = (m // tm) * nf
    n_head = (x.shape[0] if x_tail is not None else (split_rows or m)) // tm
    row_spec = pl.BlockSpec((tm, d), lambda i, f: (i, 0))
    vec_spec = pl.BlockSpec((1, d), lambda i, f: (0, 0))
    x_specs = list(_head_tail_specs((tm, d), n_head)) if x_tail is not None else [row_spec]
    x_args = [x, x_tail] if x_tail is not None else [x]
    if split_rows is not None:
        out_specs = list(_head_tail_specs((tm, d), n_head))
        out_shape = [jax.ShapeDtypeStruct((split_rows, d), F32), jax.ShapeDtypeStruct((m - split_rows, d), F32)]
    else:
        out_specs, out_shape = [row_spec], [jax.ShapeDtypeStruct((m, d), F32)]

    cast_in_specs, cast_out_specs, cast_shapes = [], [], []
    for arr, lead in cast:
        rows, cols = arr.shape[-2:]
        slab_rows = next(r for r in range(CAST_ROWS, rows + 1, CAST_ROWS)
                         if rows % r == 0 and rows // r <= n_steps)
        n_slabs = rows // slab_rows
        assert len(lead) == arr.ndim - 2
        slab = lambda i, f, n_slabs=n_slabs: jnp.minimum(i * nf + f, n_slabs - 1)
        cast_in_specs.append(pl.BlockSpec((None,) * len(lead) + (slab_rows, cols),
                                          lambda i, f, lead=lead, slab=slab: (*lead, slab(i, f), 0)))
        cast_out_specs.append(pl.BlockSpec((slab_rows, cols), lambda i, f, slab=slab: (slab(i, f), 0)))
        cast_shapes.append(jax.ShapeDtypeStruct((rows, cols), BF16))

    in_order = split_rows is not None or bool(cast)
    outs = pl.pallas_call(
        functools.partial(_ffn_body, nf=nf, n_head=n_head, pair_in=x_tail is not None,
                          pair_out=split_rows is not None, n_cast=len(cast)),
        grid=(m // tm, nf),
        in_specs=x_specs + [
            vec_spec, vec_spec,
            pl.BlockSpec((d, tf), lambda i, f: (0, f)),
            pl.BlockSpec((d, tf), lambda i, f: (0, f + nf)),
            pl.BlockSpec((tf, d), lambda i, f: (f, 0)),
        ] + cast_in_specs,
        out_specs=out_specs + cast_out_specs,
        out_shape=out_shape + cast_shapes,
        scratch_shapes=[pltpu.VMEM((tm, d), BF16)] + ([pltpu.VMEM((tm, d), F32)] if split_rows is not None else []),
        compiler_params=_params("arbitrary" if in_order else "parallel", "arbitrary"),
        name="ffn",
    )(*x_args, g_pre, g_post, w_in, w_in, w_out, *[arr for arr, _ in cast])
    n_main = len(out_shape)
    main = outs[0] if n_main == 1 else tuple(outs[:n_main])
    return main, list(outs[n_main:])


def _proj_glu_body(x_ref, g_ref, wv_ref, wg_ref, wp_ref, a_ref, p_ref):
    xn = _rms(x_ref[...], g_ref[...]).astype(BF16)
    val = jnp.dot(xn, wv_ref[...], preferred_element_type=F32)
    gate = jnp.dot(xn, wg_ref[...], preferred_element_type=F32)
    a_ref[...] = val * _sigmoid(gate)
    p_ref[...] = jnp.dot(xn, wp_ref[...], preferred_element_type=F32)


def _proj_glu(x, g, w, *, tm=WIDE_ROW_TILE):
    m, d = x.shape
    wspec = lambda part: pl.BlockSpec((d, D_CONV), lambda i: (0, part), pipeline_mode=pl.Buffered(1))
    return pl.pallas_call(
        _proj_glu_body,
        grid=(m // tm,),
        in_specs=[
            pl.BlockSpec((tm, d), lambda i: (i, 0)),
            pl.BlockSpec((1, d), lambda i: (0, 0)),
            wspec(0), wspec(1), wspec(2),
        ],
        out_specs=[
            pl.BlockSpec((tm, D_CONV), lambda i: (i, 0)),
            pl.BlockSpec((tm, D_POOL), lambda i: (i, 0)),
        ],
        out_shape=[jax.ShapeDtypeStruct((m, D_CONV), F32), jax.ShapeDtypeStruct((m, D_POOL), F32)],
        compiler_params=_params("parallel"),
        name="proj_glu",
    )(x, g, w, w, w)


def _proj_hgrn_body(x_ref, g_ref, lb_ref, wq_ref, wf_ref, wv_ref, wg_ref, q_ref, f_ref, v_ref, ga_ref, xn_ref):
    @pl.when(pl.program_id(1) == 0)
    def _():
        xn_ref[...] = _rms(x_ref[...], g_ref[...]).astype(BF16)

    xn = xn_ref[...]
    q = jnp.dot(xn, wq_ref[...], preferred_element_type=F32)
    q_ref[...] = (q * _sigmoid(q) * (DK ** -0.5)).astype(BF16)
    fz = jnp.dot(xn, wf_ref[...], preferred_element_type=F32)
    lb = lb_ref[...]
    f_ref[...] = lb + (1.0 - lb) * _sigmoid(fz)
    v_ref[...] = jnp.dot(xn, wv_ref[...], preferred_element_type=F32).astype(BF16)
    gz = jnp.dot(xn, wg_ref[...], preferred_element_type=F32)
    ga_ref[...] = (gz * _sigmoid(gz)).astype(BF16)


def _proj_hgrn(x, g, lb, w, *, tm=WIDE_ROW_TILE, tn=512):
    m, d = x.shape
    hd = HEADS * DK
    nj = hd // tn
    wspec = lambda part: pl.BlockSpec((d, tn), lambda i, j: (0, j + part * nj))
    ospec = pl.BlockSpec((tm, tn), lambda i, j: (i, j))
    act = jax.ShapeDtypeStruct((m, hd), BF16)
    return pl.pallas_call(
        _proj_hgrn_body,
        grid=(m // tm, nj),
        in_specs=[
            pl.BlockSpec((tm, d), lambda i, j: (i, 0)),
            pl.BlockSpec((1, d), lambda i, j: (0, 0)),
            pl.BlockSpec((1, tn), lambda i, j: (0, j)),
            wspec(0), wspec(1), wspec(2), wspec(3),
        ],
        out_specs=[ospec, ospec, ospec, ospec],
        out_shape=[act, jax.ShapeDtypeStruct((m, hd), F32), act, act],
        scratch_shapes=[pltpu.VMEM((tm, d), BF16)],
        compiler_params=_params("parallel", "arbitrary"),
        name="proj_hgrn",
    )(x, g, lb, w, w, w, w)


def _proj_out_body(yh_ref, yt_ref, w_ref, x_ref, g_ref, o_ref, *, n_head):
    def project(y_ref):
        m = jnp.dot(y_ref[...], w_ref[...], preferred_element_type=F32)
        o_ref[...] = x_ref[...] + _rms(m, g_ref[...])

    _by_trunk(pl.program_id(0), n_head, (yh_ref,), (yt_ref,), project)


def _proj_out(y_head, y_tail, w, x, g, *, tm=ROW_TILE):
    m, d = x.shape
    kdim = w.shape[0]
    n_head = y_head.shape[0] // tm
    return pl.pallas_call(
        functools.partial(_proj_out_body, n_head=n_head),
        grid=(m // tm,),
        in_specs=list(_head_tail_specs((tm, kdim), n_head)) + [
            pl.BlockSpec((kdim, d), lambda i: (0, 0)),
            pl.BlockSpec((tm, d), lambda i: (i, 0)),
            pl.BlockSpec((1, d), lambda i: (0, 0)),
        ],
        out_specs=pl.BlockSpec((tm, d), lambda i: (i, 0)),
        out_shape=jax.ShapeDtypeStruct((m, d), F32),
        compiler_params=_params("parallel"),
        name="proj_out",
    )(y_head, y_tail, w, x, g)


def _convpool_body(a_ref, ah_ref, ahist_ref, p_ref, ph_ref, phist_ref, cw_ref, cb_ref, lng_ref,
                   lnb_ref, pw_ref, ps_ref, o_ref, aext_ref, pext_ref, y_ref, *, t, rc, start_pos):
    i = pl.program_id(1)

    @pl.when(i == 0)
    def _():
        aext_ref[0:CONV_HALO, :] = ahist_ref[0]
        pext_ref[0:POOL_HALO, :] = phist_ref[0]

    @pl.when(i > 0)
    def _():
        aext_ref[0:CONV_HALO, :] = ah_ref[...]
        pext_ref[0:POOL_HALO, :] = ph_ref[...]

    aext_ref[CONV_HALO:CONV_HALO + t, :] = a_ref[...]
    aext_ref[CONV_HALO + t:CONV_HALO + t + SUBLANES, :] = jnp.zeros((SUBLANES, D_CONV), F32)
    pext_ref[POOL_HALO:POOL_HALO + t, :] = p_ref[...]

    lead = CONV_HALO - (CONV_WIDTH - 1)

    def conv_rows(r, carry):
        r0 = pl.multiple_of(r * rc, rc)
        for c in range(D_CONV // 128):
            cs = slice(c * 128, (c + 1) * 128)
            win = aext_ref[pl.ds(r0, rc + CONV_HALO + SUBLANES), cs]
            acc = None
            for ph in range(SUBLANES):
                part = None
                for q in range(CONV_HALO // SUBLANES + 1):
                    w = SUBLANES * q + ph - lead
                    if 0 <= w < CONV_WIDTH:
                        term = win[SUBLANES * q:SUBLANES * q + rc + SUBLANES] * cw_ref[w:w + 1, cs]
                        part = term if part is None else part + term
                shifted = part[ph:ph + rc]
                acc = shifted if acc is None else acc + shifted
            y_ref[pl.ds(r0, rc), cs] = acc + cb_ref[:, cs]
        return carry

    lax.fori_loop(0, t // rc, conv_rows, 0)

    y = y_ref[...]
    mu = jnp.mean(y, axis=-1, keepdims=True)
    yc = y - mu
    z = yc * lax.rsqrt(jnp.mean(yc * yc, axis=-1, keepdims=True) + EPS) * lng_ref[...] + lnb_ref[...]
    o_ref[:, 0:D_CONV] = (z * _sigmoid(z)).astype(BF16)

    pos = start_pos + i * t + lax.broadcasted_iota(jnp.int32, (t, 1), 0)
    for gi, win_len in enumerate(POOL_WINDOWS):
        cs = slice(gi * POOL_GROUP, (gi + 1) * POOL_GROUP)
        cur = pext_ref[POOL_HALO:POOL_HALO + t, cs]
        wsum = cur
        for dlt in range(1, win_len):
            wsum = wsum + pext_ref[POOL_HALO - dlt:POOL_HALO - dlt + t, cs]
        cnt = jnp.minimum(win_len, pos + 1).astype(F32)
        dev = (wsum / cnt - cur).astype(BF16)
        mixed = jnp.dot(dev, pw_ref[gi], preferred_element_type=F32) * ps_ref[:, cs]
        o_ref[:, D_CONV + gi * POOL_GROUP:D_CONV + (gi + 1) * POOL_GROUP] = mixed.astype(BF16)


def _convpool(a, p, a_hist, p_hist, cw, cb, lng, lnb, pw, ps, *, e, batch, seq, row_off, t, start_pos):
    nt = seq // t
    off_t, ca, cp = row_off // t, t // CONV_HALO, t // POOL_HALO
    if t >= CONV_HALO:
        ah_map = lambda b, i: (jnp.maximum((row_off // CONV_HALO) + (b * nt + i) * ca - 1, 0), 0)
        ph_map = lambda b, i: (jnp.maximum((row_off // POOL_HALO) + (b * nt + i) * cp - 1, 0), 0)
    else:
        assert nt == 1
        ah_map = lambda b, i: (0, 0)
        ph_map = lambda b, i: (0, 0)
    vec = lambda n: pl.BlockSpec((1, n), lambda b, i: (0, 0))
    rc = min(t, 64)
    return pl.pallas_call(
        functools.partial(_convpool_body, t=t, rc=rc, start_pos=start_pos),
        grid=(batch, nt),
        in_specs=[
            pl.BlockSpec((t, D_CONV), lambda b, i: (off_t + b * nt + i, 0)),
            pl.BlockSpec((CONV_HALO, D_CONV), ah_map),
            pl.BlockSpec((1, CONV_HALO, D_CONV), lambda b, i: (b, 0, 0)),
            pl.BlockSpec((t, D_POOL), lambda b, i: (off_t + b * nt + i, 0)),
            pl.BlockSpec((POOL_HALO, D_POOL), ph_map),
            pl.BlockSpec((1, POOL_HALO, D_POOL), lambda b, i: (b, 0, 0)),
            pl.BlockSpec((CONV_HALO, D_CONV), lambda b, i: (0, 0)),
            vec(D_CONV), vec(D_CONV), vec(D_CONV),
            pl.BlockSpec((None, len(POOL_WINDOWS), POOL_GROUP, POOL_GROUP), lambda b, i: (e, 0, 0, 0)),
            vec(D_POOL),
        ],
        out_specs=pl.BlockSpec((t, D_CONV + D_POOL), lambda b, i: (b * nt + i, 0)),
        out_shape=jax.ShapeDtypeStruct((batch * seq, D_CONV + D_POOL), BF16),
        scratch_shapes=[
            pltpu.VMEM((CONV_HALO + t + SUBLANES, D_CONV), F32),
            pltpu.VMEM((POOL_HALO + t, D_POOL), F32),
            pltpu.VMEM((t, D_CONV), F32),
        ],
        compiler_params=_params("parallel", "arbitrary"),
        name="convpool",
    )(a, a, a_hist, p, p, p_hist, cw, cb, lng, lnb, pw, ps)


def _hgrn_body(q_ref, f_ref, v_ref, ga_ref, gn_ref, s0_ref, o_ref, sout_ref,
               st_ref, g_ref, m_ref, tril_ref, *, c, nc):
    ci = pl.program_id(1)
    levels = [1 << b for b in range(c.bit_length() - 1)]

    @pl.when(ci == 0)
    def _():
        for h in range(HEADS):
            st_ref[h] = s0_ref[0, h].T
        ri = lax.broadcasted_iota(jnp.int32, (c, c), 0)
        cj = lax.broadcasted_iota(jnp.int32, (c, c), 1)
        tril_ref[...] = (ri >= cj).astype(BF16)
        split = jnp.where(ri > cj, ri ^ cj, 0)
        for li in range(len(levels)):
            m_ref[li] = ((split >> li) == 1).astype(BF16)

    n = c // SUBLANES
    sub = lax.broadcasted_iota(jnp.int32, (1, SUBLANES, DK), 1)
    tiles = lambda arr: arr.reshape(n, SUBLANES, DK)

    def cumulate(h):
        hs = slice(h * DK, (h + 1) * DK)
        lf = jnp.maximum(jnp.log(f_ref[:, hs]) * LOG2_E, LOG2_FLOOR)
        hi = lf.astype(BF16)
        rest = lf - hi.astype(F32)
        mid = rest.astype(BF16)
        lo = (rest - mid.astype(F32)).astype(BF16)
        sums = jnp.dot(tril_ref[...], jnp.concatenate([hi, mid, lo], axis=1), preferred_element_type=F32)
        g_ref[:, hs] = sums[:, 0:DK] + sums[:, DK:2 * DK] + sums[:, 2 * DK:3 * DK]

    def attend(h):
        hs = slice(h * DK, (h + 1) * DK)
        q = q_ref[:, hs].astype(F32)
        f = f_ref[:, hs]
        k = 1.0 - f
        vb = v_ref[:, hs]
        v = vb.astype(F32)
        g = g_ref[:, hs]
        q3, k3, f3, g3 = tiles(q), tiles(k), tiles(f), tiles(g)
        qf3 = q3 * f3

        xs = []
        for s in levels:
            if s == 1:
                x = jnp.where((sub & 1) == 1, qf3, k3)
            elif s == 2:
                f_prev = pltpu.roll(f3, 1, 1)
                f_next = pltpu.roll(f3, SUBLANES - 1, 1)
                r4 = sub & 3
                x = jnp.where(r4 == 0, k3 * f_next,
                              jnp.where(r4 == 1, k3, jnp.where(r4 == 2, qf3, qf3 * f_prev)))
            elif s == 4:
                ref = jnp.broadcast_to(g3[:, 3:4, :], g3.shape)
                upper = (sub & 4) != 0
                x = jnp.where(upper, q3, k3) * jnp.exp2(jnp.where(upper, g3 - ref, ref - g3))
            else:
                pieces = []
                for pair in range(c // (2 * s)):
                    b0 = pair * 2 * s
                    ref = g_ref[b0 + s - 1:b0 + s, hs]
                    pieces.append(k[b0:b0 + s] * jnp.exp2(ref - g[b0:b0 + s]))
                    pieces.append(q[b0 + s:b0 + 2 * s] * jnp.exp2(g[b0 + s:b0 + 2 * s] - ref))
                x = jnp.concatenate(pieces, axis=0)
            xs.append(x.reshape(c, DK).astype(BF16))

        att = None
        for li, xb in enumerate(xs):
            term = lax.dot_general(xb, xb, NT_DIMS, preferred_element_type=F32).astype(BF16) * m_ref[li]
            att = term if att is None else att + term
        return q, k, v, vb, g, att

    def finish(h, q, k, v, vb, g, att):
        hs = slice(h * DK, (h + 1) * DK)
        o = jnp.dot(att, vb, preferred_element_type=F32)
        o = o + jnp.sum(q * k, axis=-1, keepdims=True) * v

        st = st_ref[h]
        qd = (q * jnp.exp2(g)).astype(BF16)
        o = o + lax.dot_general(qd, st.astype(BF16), NT_DIMS, preferred_element_type=F32)

        g_last = g_ref[c - 1:c, hs]
        kd = (k * jnp.exp2(g_last - g)).astype(BF16)
        st_ref[h] = st * jnp.exp2(g_last) + lax.dot_general(vb, kd, TN_DIMS, preferred_element_type=F32)

        on = o * lax.rsqrt(jnp.mean(o * o, axis=-1, keepdims=True) + EPS) * gn_ref[...]
        o_ref[:, hs] = (on * ga_ref[:, hs].astype(F32)).astype(BF16)

    pending = {}
    for step in range(HEADS + 2 * HEAD_SKEW):
        if step < HEADS:
            cumulate(step)
        if 0 <= step - HEAD_SKEW < HEADS:
            pending[step - HEAD_SKEW] = attend(step - HEAD_SKEW)
        if 0 <= step - 2 * HEAD_SKEW < HEADS:
            finish(step - 2 * HEAD_SKEW, *pending.pop(step - 2 * HEAD_SKEW))

    @pl.when(ci == nc - 1)
    def _():
        for h in range(HEADS):
            sout_ref[0, h] = st_ref[h].T


def _hgrn(q, f, v, ga, gnorm, s0, *, batch, seq, row_off, c):
    nc = seq // c
    off = row_off // c
    hd = HEADS * DK
    rows = pl.BlockSpec((c, hd), lambda b, i: (off + b * nc + i, 0))
    state = pl.BlockSpec((1, HEADS, DK, DV), lambda b, i: (b, 0, 0, 0))
    n_levels = c.bit_length() - 1
    return pl.pallas_call(
        functools.partial(_hgrn_body, c=c, nc=nc),
        grid=(batch, nc),
        in_specs=[rows, rows, rows, rows, pl.BlockSpec((1, DV), lambda b, i: (0, 0)), state],
        out_specs=[pl.BlockSpec((c, hd), lambda b, i: (b * nc + i, 0)), state],
        out_shape=[
            jax.ShapeDtypeStruct((batch * seq, hd), BF16),
            jax.ShapeDtypeStruct((batch, HEADS, DK, DV), F32),
        ],
        scratch_shapes=[
            pltpu.VMEM((HEADS, DV, DK), F32),
            pltpu.VMEM((c, hd), F32),
            pltpu.VMEM((n_levels, c, c), BF16),
            pltpu.VMEM((c, c), BF16),
        ],
        compiler_params=_params("parallel", "arbitrary"),
        name="hgrn",
    )(q, f, v, ga, gnorm, s0)


def _pad_hist(h, rows):
    b, n, d = h.shape
    return jnp.concatenate([jnp.zeros((b, rows - n, d), h.dtype), h], axis=1)


def _new_cache(hist, rows, batch, seq, off, keep):
    if seq >= keep:
        return jnp.stack([rows[off + (b + 1) * seq - keep:off + (b + 1) * seq] for b in range(batch)])
    cur = rows[off:off + batch * seq].reshape(batch, seq, rows.shape[1])
    return jnp.concatenate([hist[:, seq:], cur], axis=1)


def kernel(x_prompt, x_sample, cache_conv, cache_pool, state_hgrn, ab_w_in, ab_w_out, conv_w, conv_b,
           conv_ln_g, conv_ln_b, pool_w, pool_scale, hgrn_w_in, hgrn_w_out, hgrn_gnorm, hgrn_lb, ffn_w_in,
           ffn_w_out, norm_g):
    bp, lp, d = x_prompt.shape
    bs, ls, _ = x_sample.shape
    mp, ms = bp * lp, bs * ls
    depth = ffn_w_in.shape[0]
    row = lambda vct: vct.reshape(1, -1).astype(F32)

    lb_all = jnp.cumsum(jax.nn.softmax(hgrn_lb.astype(F32), axis=0), axis=0)
    lb_all = lb_all - lb_all[0:1]

    pool_w_b = pool_w.astype(BF16)
    w_in_b, w_out_b = ffn_w_in[0, 0].astype(BF16), ffn_w_out[0, 0].astype(BF16)

    conv_p, pool_p, hgrn_p, conv_s, pool_s, hgrn_s = [], [], [], [], [], []
    x = None
    for l in range(depth):
        g = norm_g[l]
        mixer_w = (ab_w_in, ab_w_out) if l % 2 == 0 else (hgrn_w_in, hgrn_w_out)
        cast = [(ffn_w_in, (l, 1)), (ffn_w_out, (l, 1)), (mixer_w[0], (l // 2,)), (mixer_w[1], (l // 2,))]
        if l == 0:
            x, cast_w = _ffn(x_prompt.reshape(mp, d), row(g[0]), row(0.5 * g[1]), w_in_b, w_out_b,
                             x_tail=x_sample.reshape(ms, d), cast=cast)
        else:
            x, cast_w = _ffn(x, row(g[0]), row(0.5 * g[1]), w_in_b, w_out_b, cast=cast)
        w_in_b, w_out_b, mix_in_b, mix_out_b = cast_w
        if l % 2 == 0:
            e = l // 2
            a, p = _proj_glu(x, row(g[2]), mix_in_b)
            cw = jnp.concatenate([conv_w[e], jnp.zeros((CONV_HALO - CONV_WIDTH, D_CONV), F32)], axis=0)
            shared = (cw, row(conv_b[e]), row(conv_ln_g[e]), row(conv_ln_b[e]), pool_w_b, row(pool_scale[e]))
            hist_c, hist_p = cache_conv[e].astype(F32), cache_pool[e].astype(F32)
            zero_c = jnp.zeros((bp, CONV_WIDTH - 1, D_CONV), F32)
            zero_p = jnp.zeros((bp, POOL_MAX - 1, D_POOL), F32)
            y_p = _convpool(a, p, _pad_hist(zero_c, CONV_HALO), _pad_hist(zero_p, POOL_HALO), *shared,
                            e=e, batch=bp, seq=lp, row_off=0, t=512, start_pos=0)
            y_s = _convpool(a, p, _pad_hist(hist_c, CONV_HALO), _pad_hist(hist_p, POOL_HALO), *shared,
                            e=e, batch=bs, seq=ls, row_off=mp, t=ls, start_pos=PAST_LEN)
            conv_p.append(_new_cache(zero_c, a, bp, lp, 0, CONV_WIDTH - 1))
            pool_p.append(_new_cache(zero_p, p, bp, lp, 0, POOL_MAX - 1))
            conv_s.append(_new_cache(hist_c, a, bs, ls, mp, CONV_WIDTH - 1))
            pool_s.append(_new_cache(hist_p, p, bs, ls, mp, POOL_MAX - 1))
        else:
            o = l // 2
            q, f, v, ga = _proj_hgrn(x, row(g[2]), row(lb_all[l]), mix_in_b)
            gn = row(hgrn_gnorm[o])
            zero_s = jnp.zeros((bp, HEADS, DK, DV), F32)
            y_p, s_p = _hgrn(q, f, v, ga, gn, zero_s, batch=bp, seq=lp, row_off=0, c=128)
            y_s, s_s = _hgrn(q, f, v, ga, gn, state_hgrn[o].astype(F32), batch=bs, seq=ls, row_off=mp, c=ls)
            hgrn_p.append(s_p)
            hgrn_s.append(s_s.astype(state_hgrn.dtype))
        x = _proj_out(y_p, y_s, mix_out_b, x, row(g[3]))
        if l == depth - 1:
            (x_p, x_s), _ = _ffn(x, row(g[4]), row(0.5 * g[5]), w_in_b, w_out_b, split_rows=mp)
        else:
            x, (w_in_b, w_out_b) = _ffn(x, row(g[4]), row(0.5 * g[5]), w_in_b, w_out_b,
                                        cast=[(ffn_w_in, (l + 1, 0)), (ffn_w_out, (l + 1, 0))])

    return (x_p.reshape(bp, lp, d), x_s.reshape(bs, ls, d),
            jnp.stack(conv_p), jnp.stack(pool_p), jnp.stack(hgrn_p),
            jnp.stack(conv_s), jnp.stack(pool_s), jnp.stack(hgrn_s))
```

```python
import functools

import jax
import jax.numpy as jnp
from jax import lax
from jax.experimental import pallas as pl
from jax.experimental.pallas import tpu as pltpu

F32 = jnp.float32
BF16 = jnp.bfloat16

D_MODEL = 2048
D_CONV = 1024
D_POOL = 1024
CONV_WIDTH = 31
POOL_WINDOWS = (2, 4, 8, 16)
POOL_GROUP = 256
POOL_MAX = 16
HEADS = 16
DK = 128
DV = 128
D_FF = 5632
EPS = 1e-6
PAST_LEN = 4096
LOG2_E = 1.4426950408889634
LOG2_FLOOR = -150.0

VMEM_LIMIT_BYTES = 54 * 1024 * 1024
ROW_TILE = 512
WIDE_ROW_TILE = 768
SUBLANES = 8
NORM_ROWS = 32
CAST_ROWS = 16
CONV_HALO = 32
POOL_HALO = 16
HEAD_SKEW = 2
NT_DIMS = (((1,), (1,)), ((), ()))
TN_DIMS = (((0,), (0,)), ((), ()))


def _rms(x, g):
    return x * lax.rsqrt(jnp.mean(x * x, axis=-1, keepdims=True) + EPS) * g


def _sigmoid(x):
    return 1.0 / (1.0 + jnp.exp(-x))


def _params(*sem):
    return pltpu.CompilerParams(dimension_semantics=sem, vmem_limit_bytes=VMEM_LIMIT_BYTES)


def _head_tail_specs(shape, n_head):
    head = pl.BlockSpec(shape, lambda i, *_: (jnp.minimum(i, n_head - 1), 0))
    tail = pl.BlockSpec(shape, lambda i, *_: (jnp.maximum(i - n_head, 0), 0))
    return head, tail


def _by_trunk(i, n_head, head_refs, tail_refs, fn):
    @pl.when(i < n_head)
    def _():
        fn(*head_refs)

    @pl.when(i >= n_head)
    def _():
        fn(*tail_refs)


def _ffn_body(*refs, nf, last, cut, pair_in, pair_out, n_cast):
    refs = list(refs)
    xh_ref = refs.pop(0)
    xt_ref = refs.pop(0) if pair_in else None
    g0_ref, g1_ref, wa_ref, wb_ref, wo_ref = (refs.pop(0) for _ in range(5))
    cast_in = [refs.pop(0) for _ in range(n_cast)]
    oh_ref = refs.pop(0)
    ot_ref = refs.pop(0) if pair_out else None
    cast_out = [refs.pop(0) for _ in range(n_cast)]
    (xn_ref,) = refs
    acc_ref = oh_ref
    tm = xn_ref.shape[0]
    i, f = pl.program_id(0), pl.program_id(1)

    for src_ref, dst_ref in zip(cast_in, cast_out):
        dst_ref[...] = src_ref[...].astype(BF16)

    def per_tile(fn):
        if pair_in or pair_out:
            pl.when(i < last)(functools.partial(fn, False))
            pl.when(i == last)(functools.partial(fn, True))
        else:
            fn(False)

    @pl.when(f == 0)
    def _():
        def prologue(straddling):
            if straddling and pair_in:
                xn_ref[0:cut, :] = _rms(xh_ref[0:cut, :], g0_ref[...]).astype(BF16)
                xn_ref[cut:tm, :] = _rms(xt_ref[...], g0_ref[...]).astype(BF16)
            else:
                xn_ref[...] = _rms(xh_ref[...], g0_ref[...]).astype(BF16)

        per_tile(prologue)
        acc_ref[...] = jnp.zeros(acc_ref.shape, F32)

    xn = xn_ref[...]
    a = jnp.dot(xn, wa_ref[...], preferred_element_type=F32)
    b = jnp.dot(xn, wb_ref[...], preferred_element_type=F32)
    h = (a * _sigmoid(a) * b).astype(BF16)
    acc_ref[...] += jnp.dot(h, wo_ref[...], preferred_element_type=F32)

    @pl.when(f == nf - 1)
    def _():
        def epilogue(straddling):
            for r0 in range(0, tm, NORM_ROWS):
                r = slice(r0, r0 + NORM_ROWS)
                rt = slice(r0 - cut, r0 - cut + NORM_ROWS)
                in_tail = straddling and r0 >= cut
                x = xt_ref[rt, :] if (in_tail and pair_in) else xh_ref[r, :]
                y = x + _rms(acc_ref[r, :], g1_ref[...])
                if in_tail and pair_out:
                    ot_ref[rt, :] = y
                else:
                    oh_ref[r, :] = y

        per_tile(epilogue)


def _ffn(x, g_pre, g_post, w_in, w_out, *, x_tail=None, split_rows=None, cast=(), tm=WIDE_ROW_TILE, tf=512):
    d = x.shape[1]
    m = x.shape[0] + (0 if x_tail is None else x_tail.shape[0])
    nf = D_FF // tf
    n_tiles = m // tm
    n_steps = n_tiles * nf
    n_head_rows = x.shape[0] if x_tail is not None else (split_rows or m)
    cut = n_head_rows - (n_tiles - 1) * tm
    assert m % tm == 0 and 0 < cut <= tm and cut % NORM_ROWS == 0 and tm % NORM_ROWS == 0
    row_spec = pl.BlockSpec((tm, d), lambda i, f: (i, 0))
    tail_spec = pl.BlockSpec((m - n_head_rows, d), lambda i, f: (0, 0))
    vec_spec = pl.BlockSpec((1, d), lambda i, f: (0, 0))
    x_specs = [row_spec, tail_spec] if x_tail is not None else [row_spec]
    x_args = [x, x_tail] if x_tail is not None else [x]
    if split_rows is not None:
        out_specs = [row_spec, tail_spec]
        out_shape = [jax.ShapeDtypeStruct((split_rows, d), F32), jax.ShapeDtypeStruct((m - split_rows, d), F32)]
    else:
        out_specs, out_shape = [row_spec], [jax.ShapeDtypeStruct((m, d), F32)]

    cast_in_specs, cast_out_specs, cast_shapes = [], [], []
    for arr, lead in cast:
        rows, cols = arr.shape[-2:]
        slab_rows = next(r for r in range(CAST_ROWS, rows + 1, CAST_ROWS)
                         if rows % r == 0 and rows // r <= n_steps)
        n_slabs = rows // slab_rows
        assert len(lead) == arr.ndim - 2
        slab = lambda i, f, n_slabs=n_slabs: jnp.minimum(i * nf + f, n_slabs - 1)
        cast_in_specs.append(pl.BlockSpec((None,) * len(lead) + (slab_rows, cols),
                                          lambda i, f, lead=lead, slab=slab: (*lead, slab(i, f), 0)))
        cast_out_specs.append(pl.BlockSpec((slab_rows, cols), lambda i, f, slab=slab: (slab(i, f), 0)))
        cast_shapes.append(jax.ShapeDtypeStruct((rows, cols), BF16))

    in_order = split_rows is not None or bool(cast)
    outs = pl.pallas_call(
        functools.partial(_ffn_body, nf=nf, last=n_tiles - 1, cut=cut, pair_in=x_tail is not None,
                          pair_out=split_rows is not None, n_cast=len(cast)),
        grid=(m // tm, nf),
        in_specs=x_specs + [
            vec_spec, vec_spec,
            pl.BlockSpec((d, tf), lambda i, f: (0, f)),
            pl.BlockSpec((d, tf), lambda i, f: (0, f + nf)),
            pl.BlockSpec((tf, d), lambda i, f: (f, 0)),
        ] + cast_in_specs,
        out_specs=out_specs + cast_out_specs,
        out_shape=out_shape + cast_shapes,
        scratch_shapes=[pltpu.VMEM((tm, d), BF16)],
        compiler_params=_params("arbitrary" if in_order else "parallel", "arbitrary"),
        name="ffn",
    )(*x_args, g_pre, g_post, w_in, w_in, w_out, *[arr for arr, _ in cast])
    n_main = len(out_shape)
    main = outs[0] if n_main == 1 else tuple(outs[:n_main])
    return main, list(outs[n_main:])


def _proj_glu_body(x_ref, g_ref, wv_ref, wg_ref, wp_ref, a_ref, p_ref):
    xn = _rms(x_ref[...], g_ref[...]).astype(BF16)
    val = jnp.dot(xn, wv_ref[...], preferred_element_type=F32)
    gate = jnp.dot(xn, wg_ref[...], preferred_element_type=F32)
    a_ref[...] = val * _sigmoid(gate)
    p_ref[...] = jnp.dot(xn, wp_ref[...], preferred_element_type=F32)


def _proj_glu(x, g, w, *, tm=WIDE_ROW_TILE):
    m, d = x.shape
    wspec = lambda part: pl.BlockSpec((d, D_CONV), lambda i: (0, part), pipeline_mode=pl.Buffered(1))
    return pl.pallas_call(
        _proj_glu_body,
        grid=(m // tm,),
        in_specs=[
            pl.BlockSpec((tm, d), lambda i: (i, 0)),
            pl.BlockSpec((1, d), lambda i: (0, 0)),
            wspec(0), wspec(1), wspec(2),
        ],
        out_specs=[
            pl.BlockSpec((tm, D_CONV), lambda i: (i, 0)),
            pl.BlockSpec((tm, D_POOL), lambda i: (i, 0)),
        ],
        out_shape=[jax.ShapeDtypeStruct((m, D_CONV), F32), jax.ShapeDtypeStruct((m, D_POOL), F32)],
        compiler_params=_params("parallel"),
        name="proj_glu",
    )(x, g, w, w, w)


def _proj_hgrn_body(x_ref, g_ref, lb_ref, wq_ref, wf_ref, wv_ref, wg_ref, q_ref, f_ref, v_ref, ga_ref, xn_ref):
    @pl.when(pl.program_id(1) == 0)
    def _():
        xn_ref[...] = _rms(x_ref[...], g_ref[...]).astype(BF16)

    xn = xn_ref[...]
    q = jnp.dot(xn, wq_ref[...], preferred_element_type=F32)
    q_ref[...] = (q * _sigmoid(q) * (DK ** -0.5)).astype(BF16)
    fz = jnp.dot(xn, wf_ref[...], preferred_element_type=F32)
    lb = lb_ref[...]
    f_ref[...] = lb + (1.0 - lb) * _sigmoid(fz)
    v_ref[...] = jnp.dot(xn, wv_ref[...], preferred_element_type=F32).astype(BF16)
    gz = jnp.dot(xn, wg_ref[...], preferred_element_type=F32)
    ga_ref[...] = (gz * _sigmoid(gz)).astype(BF16)


def _proj_hgrn(x, g, lb, w, *, tm=WIDE_ROW_TILE, tn=512):
    m, d = x.shape
    hd = HEADS * DK
    nj = hd // tn
    wspec = lambda part: pl.BlockSpec((d, tn), lambda i, j: (0, j + part * nj))
    ospec = pl.BlockSpec((tm, tn), lambda i, j: (i, j))
    act = jax.ShapeDtypeStruct((m, hd), BF16)
    return pl.pallas_call(
        _proj_hgrn_body,
        grid=(m // tm, nj),
        in_specs=[
            pl.BlockSpec((tm, d), lambda i, j: (i, 0)),
            pl.BlockSpec((1, d), lambda i, j: (0, 0)),
            pl.BlockSpec((1, tn), lambda i, j: (0, j)),
            wspec(0), wspec(1), wspec(2), wspec(3),
        ],
        out_specs=[ospec, ospec, ospec, ospec],
        out_shape=[act, jax.ShapeDtypeStruct((m, hd), F32), act, act],
        scratch_shapes=[pltpu.VMEM((tm, d), BF16)],
        compiler_params=_params("parallel", "arbitrary"),
        name="proj_hgrn",
    )(x, g, lb, w, w, w, w)


def _proj_out_body(yh_ref, yt_ref, w_ref, x_ref, g_ref, o_ref, *, n_head):
    def project(y_ref):
        m = jnp.dot(y_ref[...], w_ref[...], preferred_element_type=F32)
        o_ref[...] = x_ref[...] + _rms(m, g_ref[...])

    _by_trunk(pl.program_id(0), n_head, (yh_ref,), (yt_ref,), project)


def _proj_out(y_head, y_tail, w, x, g, *, tm=ROW_TILE):
    m, d = x.shape
    kdim = w.shape[0]
    n_head = y_head.shape[0] // tm
    return pl.pallas_call(
        functools.partial(_proj_out_body, n_head=n_head),
        grid=(m // tm,),
        in_specs=list(_head_tail_specs((tm, kdim), n_head)) + [
            pl.BlockSpec((kdim, d), lambda i: (0, 0)),
            pl.BlockSpec((tm, d), lambda i: (i, 0)),
            pl.BlockSpec((1, d), lambda i: (0, 0)),
        ],
        out_specs=pl.BlockSpec((tm, d), lambda i: (i, 0)),
        out_shape=jax.ShapeDtypeStruct((m, d), F32),
        compiler_params=_params("parallel"),
        name="proj_out",
    )(y_head, y_tail, w, x, g)


def _convpool_body(a_ref, ah_ref, ahist_ref, p_ref, ph_ref, phist_ref, cw_ref, cb_ref, lng_ref,
                   lnb_ref, pw_ref, ps_ref, o_ref, aext_ref, pext_ref, y_ref, *, t, rc, start_pos):
    i = pl.program_id(1)

    @pl.when(i == 0)
    def _():
        aext_ref[0:CONV_HALO, :] = ahist_ref[0]
        pext_ref[0:POOL_HALO, :] = phist_ref[0]

    @pl.when(i > 0)
    def _():
        aext_ref[0:CONV_HALO, :] = ah_ref[...]
        pext_ref[0:POOL_HALO, :] = ph_ref[...]

    aext_ref[CONV_HALO:CONV_HALO + t, :] = a_ref[...]
    aext_ref[CONV_HALO + t:CONV_HALO + t + SUBLANES, :] = jnp.zeros((SUBLANES, D_CONV), F32)
    pext_ref[POOL_HALO:POOL_HALO + t, :] = p_ref[...]

    lead = CONV_HALO - (CONV_WIDTH - 1)

    def conv_rows(r, carry):
        r0 = pl.multiple_of(r * rc, rc)
        for c in range(D_CONV // 128):
            cs = slice(c * 128, (c + 1) * 128)
            win = aext_ref[pl.ds(r0, rc + CONV_HALO + SUBLANES), cs]
            acc = None
            for ph in range(SUBLANES):
                part = None
                for q in range(CONV_HALO // SUBLANES + 1):
                    w = SUBLANES * q + ph - lead
                    if 0 <= w < CONV_WIDTH:
                        term = win[SUBLANES * q:SUBLANES * q + rc + SUBLANES] * cw_ref[w:w + 1, cs]
                        part = term if part is None else part + term
                shifted = part[ph:ph + rc]
                acc = shifted if acc is None else acc + shifted
            y_ref[pl.ds(r0, rc), cs] = acc + cb_ref[:, cs]
        return carry

    lax.fori_loop(0, t // rc, conv_rows, 0)

    y = y_ref[...]
    mu = jnp.mean(y, axis=-1, keepdims=True)
    yc = y - mu
    z = yc * lax.rsqrt(jnp.mean(yc * yc, axis=-1, keepdims=True) + EPS) * lng_ref[...] + lnb_ref[...]
    o_ref[:, 0:D_CONV] = (z * _sigmoid(z)).astype(BF16)

    pos = start_pos + i * t + lax.broadcasted_iota(jnp.int32, (t, 1), 0)
    for gi, win_len in enumerate(POOL_WINDOWS):
        cs = slice(gi * POOL_GROUP, (gi + 1) * POOL_GROUP)
        cur = pext_ref[POOL_HALO:POOL_HALO + t, cs]
        wsum = cur
        for dlt in range(1, win_len):
            wsum = wsum + pext_ref[POOL_HALO - dlt:POOL_HALO - dlt + t, cs]
        cnt = jnp.minimum(win_len, pos + 1).astype(F32)
        dev = (wsum / cnt - cur).astype(BF16)
        mixed = jnp.dot(dev, pw_ref[gi], preferred_element_type=F32) * ps_ref[:, cs]
        o_ref[:, D_CONV + gi * POOL_GROUP:D_CONV + (gi + 1) * POOL_GROUP] = mixed.astype(BF16)


def _convpool(a, p, a_hist, p_hist, cw, cb, lng, lnb, pw, ps, *, e, batch, seq, row_off, t, start_pos):
    nt = seq // t
    off_t, ca, cp = row_off // t, t // CONV_HALO, t // POOL_HALO
    if t >= CONV_HALO:
        ah_map = lambda b, i: (jnp.maximum((row_off // CONV_HALO) + (b * nt + i) * ca - 1, 0), 0)
        ph_map = lambda b, i: (jnp.maximum((row_off // POOL_HALO) + (b * nt + i) * cp - 1, 0), 0)
    else:
        assert nt == 1
        ah_map = lambda b, i: (0, 0)
        ph_map = lambda b, i: (0, 0)
    vec = lambda n: pl.BlockSpec((1, n), lambda b, i: (0, 0))
    rc = min(t, 64)
    return pl.pallas_call(
        functools.partial(_convpool_body, t=t, rc=rc, start_pos=start_pos),
        grid=(batch, nt),
        in_specs=[
            pl.BlockSpec((t, D_CONV), lambda b, i: (off_t + b * nt + i, 0)),
            pl.BlockSpec((CONV_HALO, D_CONV), ah_map),
            pl.BlockSpec((1, CONV_HALO, D_CONV), lambda b, i: (b, 0, 0)),
            pl.BlockSpec((t, D_POOL), lambda b, i: (off_t + b * nt + i, 0)),
            pl.BlockSpec((POOL_HALO, D_POOL), ph_map),
            pl.BlockSpec((1, POOL_HALO, D_POOL), lambda b, i: (b, 0, 0)),
            pl.BlockSpec((CONV_HALO, D_CONV), lambda b, i: (0, 0)),
            vec(D_CONV), vec(D_CONV), vec(D_CONV),
            pl.BlockSpec((None, len(POOL_WINDOWS), POOL_GROUP, POOL_GROUP), lambda b, i: (e, 0, 0, 0)),
            vec(D_POOL),
        ],
        out_specs=pl.BlockSpec((t, D_CONV + D_POOL), lambda b, i: (b * nt + i, 0)),
        out_shape=jax.ShapeDtypeStruct((batch * seq, D_CONV + D_POOL), BF16),
        scratch_shapes=[
            pltpu.VMEM((CONV_HALO + t + SUBLANES, D_CONV), F32),
            pltpu.VMEM((POOL_HALO + t, D_POOL), F32),
            pltpu.VMEM((t, D_CONV), F32),
        ],
        compiler_params=_params("parallel", "arbitrary"),
        name="convpool",
    )(a, a, a_hist, p, p, p_hist, cw, cb, lng, lnb, pw, ps)


def _hgrn_body(q_ref, f_ref, v_ref, ga_ref, gn_ref, s0_ref, o_ref, sout_ref,
               st_ref, g_ref, m_ref, tril_ref, *, c, nc):
    ci = pl.program_id(1)
    levels = [1 << b for b in range(c.bit_length() - 1)]

    @pl.when(ci == 0)
    def _():
        for h in range(HEADS):
            st_ref[h] = s0_ref[0, h].T
        ri = lax.broadcasted_iota(jnp.int32, (c, c), 0)
        cj = lax.broadcasted_iota(jnp.int32, (c, c), 1)
        tril_ref[...] = (ri >= cj).astype(BF16)
        split = jnp.where(ri > cj, ri ^ cj, 0)
        for li in range(len(levels)):
            m_ref[li] = ((split >> li) == 1).astype(BF16)

    n = c // SUBLANES
    sub = lax.broadcasted_iota(jnp.int32, (1, SUBLANES, DK), 1)
    tiles = lambda arr: arr.reshape(n, SUBLANES, DK)

    def cumulate(h):
        hs = slice(h * DK, (h + 1) * DK)
        lf = jnp.maximum(jnp.log(f_ref[:, hs]) * LOG2_E, LOG2_FLOOR)
        hi = lf.astype(BF16)
        rest = lf - hi.astype(F32)
        mid = rest.astype(BF16)
        lo = (rest - mid.astype(F32)).astype(BF16)
        sums = jnp.dot(tril_ref[...], jnp.concatenate([hi, mid, lo], axis=1), preferred_element_type=F32)
        g_ref[:, hs] = sums[:, 0:DK] + sums[:, DK:2 * DK] + sums[:, 2 * DK:3 * DK]

    def attend(h):
        hs = slice(h * DK, (h + 1) * DK)
        q = q_ref[:, hs].astype(F32)
        f = f_ref[:, hs]
        k = 1.0 - f
        vb = v_ref[:, hs]
        v = vb.astype(F32)
        g = g_ref[:, hs]
        q3, k3, f3, g3 = tiles(q), tiles(k), tiles(f), tiles(g)
        qf3 = q3 * f3

        xs = []
        for s in levels:
            if s == 1:
                x = jnp.where((sub & 1) == 1, qf3, k3)
            elif s == 2:
                f_prev = pltpu.roll(f3, 1, 1)
                f_next = pltpu.roll(f3, SUBLANES - 1, 1)
                r4 = sub & 3
                x = jnp.where(r4 == 0, k3 * f_next,
                              jnp.where(r4 == 1, k3, jnp.where(r4 == 2, qf3, qf3 * f_prev)))
            elif s == 4:
                ref = jnp.broadcast_to(g3[:, 3:4, :], g3.shape)
                upper = (sub & 4) != 0
                x = jnp.where(upper, q3, k3) * jnp.exp2(jnp.where(upper, g3 - ref, ref - g3))
            else:
                pieces = []
                for pair in range(c // (2 * s)):
                    b0 = pair * 2 * s
                    ref = g_ref[b0 + s - 1:b0 + s, hs]
                    pieces.append(k[b0:b0 + s] * jnp.exp2(ref - g[b0:b0 + s]))
                    pieces.append(q[b0 + s:b0 + 2 * s] * jnp.exp2(g[b0 + s:b0 + 2 * s] - ref))
                x = jnp.concatenate(pieces, axis=0)
            xs.append(x.reshape(c, DK).astype(BF16))

        att = None
        for li, xb in enumerate(xs):
            term = lax.dot_general(xb, xb, NT_DIMS, preferred_element_type=F32).astype(BF16) * m_ref[li]
            att = term if att is None else att + term
        return q, k, v, vb, g, att

    def finish(h, q, k, v, vb, g, att):
        hs = slice(h * DK, (h + 1) * DK)
        o = jnp.dot(att, vb, preferred_element_type=F32)
        o = o + jnp.sum(q * k, axis=-1, keepdims=True) * v

        st = st_ref[h]
        qd = (q * jnp.exp2(g)).astype(BF16)
        o = o + lax.dot_general(qd, st.astype(BF16), NT_DIMS, preferred_element_type=F32)

        g_last = g_ref[c - 1:c, hs]
        kd = (k * jnp.exp2(g_last - g)).astype(BF16)
        st_ref[h] = st * jnp.exp2(g_last) + lax.dot_general(vb, kd, TN_DIMS, preferred_element_type=F32)

        on = o * lax.rsqrt(jnp.mean(o * o, axis=-1, keepdims=True) + EPS) * gn_ref[...]
        o_ref[:, hs] = (on * ga_ref[:, hs].astype(F32)).astype(BF16)

    pending = {}
    for step in range(HEADS + 2 * HEAD_SKEW):
        if step < HEADS:
            cumulate(step)
        if 0 <= step - HEAD_SKEW < HEADS:
            pending[step - HEAD_SKEW] = attend(step - HEAD_SKEW)
        if 0 <= step - 2 * HEAD_SKEW < HEADS:
            finish(step - 2 * HEAD_SKEW, *pending.pop(step - 2 * HEAD_SKEW))

    @pl.when(ci == nc - 1)
    def _():
        for h in range(HEADS):
            sout_ref[0, h] = st_ref[h].T


def _hgrn(q, f, v, ga, gnorm, s0, *, batch, seq, row_off, c):
    nc = seq // c
    off = row_off // c
    hd = HEADS * DK
    rows = pl.BlockSpec((c, hd), lambda b, i: (off + b * nc + i, 0))
    state = pl.BlockSpec((1, HEADS, DK, DV), lambda b, i: (b, 0, 0, 0))
    n_levels = c.bit_length() - 1
    return pl.pallas_call(
        functools.partial(_hgrn_body, c=c, nc=nc),
        grid=(batch, nc),
        in_specs=[rows, rows, rows, rows, pl.BlockSpec((1, DV), lambda b, i: (0, 0)), state],
        out_specs=[pl.BlockSpec((c, hd), lambda b, i: (b * nc + i, 0)), state],
        out_shape=[
            jax.ShapeDtypeStruct((batch * seq, hd), BF16),
            jax.ShapeDtypeStruct((batch, HEADS, DK, DV), F32),
        ],
        scratch_shapes=[
            pltpu.VMEM((HEADS, DV, DK), F32),
            pltpu.VMEM((c, hd), F32),
            pltpu.VMEM((n_levels, c, c), BF16),
            pltpu.VMEM((c, c), BF16),
        ],
        compiler_params=_params("parallel", "arbitrary"),
        name="hgrn",
    )(q, f, v, ga, gnorm, s0)


def _pad_hist(h, rows):
    b, n, d = h.shape
    return jnp.concatenate([jnp.zeros((b, rows - n, d), h.dtype), h], axis=1)


def _new_cache(hist, rows, batch, seq, off, keep):
    if seq >= keep:
        return jnp.stack([rows[off + (b + 1) * seq - keep:off + (b + 1) * seq] for b in range(batch)])
    cur = rows[off:off + batch * seq].reshape(batch, seq, rows.shape[1])
    return jnp.concatenate([hist[:, seq:], cur], axis=1)


def kernel(x_prompt, x_sample, cache_conv, cache_pool, state_hgrn, ab_w_in, ab_w_out, conv_w, conv_b,
           conv_ln_g, conv_ln_b, pool_w, pool_scale, hgrn_w_in, hgrn_w_out, hgrn_gnorm, hgrn_lb, ffn_w_in,
           ffn_w_out, norm_g):
    bp, lp, d = x_prompt.shape
    bs, ls, _ = x_sample.shape
    mp, ms = bp * lp, bs * ls
    depth = ffn_w_in.shape[0]
    row = lambda vct: vct.reshape(1, -1).astype(F32)

    lb_all = jnp.cumsum(jax.nn.softmax(hgrn_lb.astype(F32), axis=0), axis=0)
    lb_all = lb_all - lb_all[0:1]

    pool_w_b = pool_w.astype(BF16)
    w_in_b, w_out_b = ffn_w_in[0, 0].astype(BF16), ffn_w_out[0, 0].astype(BF16)

    conv_p, pool_p, hgrn_p, conv_s, pool_s, hgrn_s = [], [], [], [], [], []
    x = None
    for l in range(depth):
        g = norm_g[l]
        mixer_w = (ab_w_in, ab_w_out) if l % 2 == 0 else (hgrn_w_in, hgrn_w_out)
        cast = [(ffn_w_in, (l, 1)), (ffn_w_out, (l, 1)), (mixer_w[0], (l // 2,)), (mixer_w[1], (l // 2,))]
        if l == 0:
            x, cast_w = _ffn(x_prompt.reshape(mp, d), row(g[0]), row(0.5 * g[1]), w_in_b, w_out_b,
                             x_tail=x_sample.reshape(ms, d), cast=cast)
        else:
            x, cast_w = _ffn(x, row(g[0]), row(0.5 * g[1]), w_in_b, w_out_b, cast=cast)
        w_in_b, w_out_b, mix_in_b, mix_out_b = cast_w
        if l % 2 == 0:
            e = l // 2
            a, p = _proj_glu(x, row(g[2]), mix_in_b)
            cw = jnp.concatenate([conv_w[e], jnp.zeros((CONV_HALO - CONV_WIDTH, D_CONV), F32)], axis=0)
            shared = (cw, row(conv_b[e]), row(conv_ln_g[e]), row(conv_ln_b[e]), pool_w_b, row(pool_scale[e]))
            hist_c, hist_p = cache_conv[e].astype(F32), cache_pool[e].astype(F32)
            zero_c = jnp.zeros((bp, CONV_WIDTH - 1, D_CONV), F32)
            zero_p = jnp.zeros((bp, POOL_MAX - 1, D_POOL), F32)
            y_p = _convpool(a, p, _pad_hist(zero_c, CONV_HALO), _pad_hist(zero_p, POOL_HALO), *shared,
                            e=e, batch=bp, seq=lp, row_off=0, t=512, start_pos=0)
            y_s = _convpool(a, p, _pad_hist(hist_c, CONV_HALO), _pad_hist(hist_p, POOL_HALO), *shared,
                            e=e, batch=bs, seq=ls, row_off=mp, t=ls, start_pos=PAST_LEN)
            conv_p.append(_new_cache(zero_c, a, bp, lp, 0, CONV_WIDTH - 1))
            pool_p.append(_new_cache(zero_p, p, bp, lp, 0, POOL_MAX - 1))
            conv_s.append(_new_cache(hist_c, a, bs, ls, mp, CONV_WIDTH - 1))
            pool_s.append(_new_cache(hist_p, p, bs, ls, mp, POOL_MAX - 1))
        else:
            o = l // 2
            q, f, v, ga = _proj_hgrn(x, row(g[2]), row(lb_all[l]), mix_in_b)
            gn = row(hgrn_gnorm[o])
            zero_s = jnp.zeros((bp, HEADS, DK, DV), F32)
            y_p, s_p = _hgrn(q, f, v, ga, gn, zero_s, batch=bp, seq=lp, row_off=0, c=128)
            y_s, s_s = _hgrn(q, f, v, ga, gn, state_hgrn[o].astype(F32), batch=bs, seq=ls, row_off=mp, c=ls)
            hgrn_p.append(s_p)
            hgrn_s.append(s_s.astype(state_hgrn.dtype))
        x = _proj_out(y_p, y_s, mix_out_b, x, row(g[3]))
        if l == depth - 1:
            (x_p, x_s), _ = _ffn(x, row(g[4]), row(0.5 * g[5]), w_in_b, w_out_b, split_rows=mp)
        else:
            x, (w_in_b, w_out_b) = _ffn(x, row(g[4]), row(0.5 * g[5]), w_in_b, w_out_b,
                                        cast=[(ffn_w_in, (l + 1, 0)), (ffn_w_out, (l + 1, 0))])

    return (x_p.reshape(bp, lp, d), x_s.reshape(bs, ls, d),
            jnp.stack(conv_p), jnp.stack(pool_p), jnp.stack(hgrn_p),
            jnp.stack(conv_s), jnp.stack(pool_s), jnp.stack(hgrn_s))
```

```python
import functools

import jax
import jax.numpy as jnp
from jax import lax
from jax.experimental import pallas as pl
from jax.experimental.pallas import tpu as pltpu

F32 = jnp.float32
BF16 = jnp.bfloat16

D_MODEL = 2048
D_CONV = 1024
D_POOL = 1024
CONV_WIDTH = 31
POOL_WINDOWS = (2, 4, 8, 16)
POOL_GROUP = 256
POOL_MAX = 16
HEADS = 16
DK = 128
DV = 128
D_FF = 5632
EPS = 1e-6
PAST_LEN = 4096
LOG2_E = 1.4426950408889634
LOG2_FLOOR = -150.0

VMEM_LIMIT_BYTES = 54 * 1024 * 1024
ROW_TILE = 512
WIDE_ROW_TILE = 768
SUBLANES = 8
NORM_ROWS = 32
CAST_ROWS = 16
CONV_HALO = 32
POOL_HALO = 16
HEAD_SKEW = 2
NT_DIMS = (((1,), (1,)), ((), ()))
TN_DIMS = (((0,), (0,)), ((), ()))


def _rms(x, g):
    return x * lax.rsqrt(jnp.mean(x * x, axis=-1, keepdims=True) + EPS) * g


def _sigmoid(x):
    return 1.0 / (1.0 + jnp.exp(-x))


def _params(*sem):
    return pltpu.CompilerParams(dimension_semantics=sem, vmem_limit_bytes=VMEM_LIMIT_BYTES)


def _head_tail_specs(shape, n_head):
    head = pl.BlockSpec(shape, lambda i, *_: (jnp.minimum(i, n_head - 1), 0))
    tail = pl.BlockSpec(shape, lambda i, *_: (jnp.maximum(i - n_head, 0), 0))
    return head, tail


def _by_trunk(i, n_head, head_refs, tail_refs, fn):
    @pl.when(i < n_head)
    def _():
        fn(*head_refs)

    @pl.when(i >= n_head)
    def _():
        fn(*tail_refs)


def _ffn_body(*refs, nf, last, cut, pair_in, pair_out, n_cast):
    refs = list(refs)
    xh_ref = refs.pop(0)
    xt_ref = refs.pop(0) if pair_in else None
    g0_ref, g1_ref, wa_ref, wb_ref, wo_ref = (refs.pop(0) for _ in range(5))
    cast_in = [refs.pop(0) for _ in range(n_cast)]
    oh_ref = refs.pop(0)
    ot_ref = refs.pop(0) if pair_out else None
    cast_out = [refs.pop(0) for _ in range(n_cast)]
    (xn_ref,) = refs
    acc_ref = oh_ref
    tm = xn_ref.shape[0]
    i, f = pl.program_id(0), pl.program_id(1)

    for src_ref, dst_ref in zip(cast_in, cast_out):
        dst_ref[...] = src_ref[...].astype(BF16)

    def per_tile(fn):
        if pair_in or pair_out:
            pl.when(i < last)(functools.partial(fn, False))
            pl.when(i == last)(functools.partial(fn, True))
        else:
            fn(False)

    @pl.when(f == 0)
    def _():
        def prologue(straddling):
            if straddling and pair_in:
                xn_ref[0:cut, :] = _rms(xh_ref[0:cut, :], g0_ref[...]).astype(BF16)
                xn_ref[cut:tm, :] = _rms(xt_ref[...], g0_ref[...]).astype(BF16)
            else:
                xn_ref[...] = _rms(xh_ref[...], g0_ref[...]).astype(BF16)

        per_tile(prologue)
        acc_ref[...] = jnp.zeros(acc_ref.shape, F32)

    xn = xn_ref[...]
    a = jnp.dot(xn, wa_ref[...], preferred_element_type=F32)
    b = jnp.dot(xn, wb_ref[...], preferred_element_type=F32)
    h = (a * _sigmoid(a) * b).astype(BF16)
    acc_ref[...] += jnp.dot(h, wo_ref[...], preferred_element_type=F32)

    @pl.when(f == nf - 1)
    def _():
        def epilogue(straddling):
            for r0 in range(0, tm, NORM_ROWS):
                r = slice(r0, r0 + NORM_ROWS)
                rt = slice(r0 - cut, r0 - cut + NORM_ROWS)
                in_tail = straddling and r0 >= cut
                x = xt_ref[rt, :] if (in_tail and pair_in) else xh_ref[r, :]
                y = x + _rms(acc_ref[r, :], g1_ref[...])
                if in_tail and pair_out:
                    ot_ref[rt, :] = y
                else:
                    oh_ref[r, :] = y

        per_tile(epilogue)


def _ffn(x, g_pre, g_post, w_in, w_out, *, x_tail=None, split_rows=None, cast=(), tm=WIDE_ROW_TILE, tf=512):
    d = x.shape[1]
    m = x.shape[0] + (0 if x_tail is None else x_tail.shape[0])
    nf = D_FF // tf
    n_tiles = m // tm
    n_steps = n_tiles * nf
    n_head_rows = x.shape[0] if x_tail is not None else (split_rows or m)
    cut = n_head_rows - (n_tiles - 1) * tm
    assert m % tm == 0 and 0 < cut <= tm and cut % NORM_ROWS == 0 and tm % NORM_ROWS == 0
    row_spec = pl.BlockSpec((tm, d), lambda i, f: (i, 0))
    tail_spec = pl.BlockSpec((m - n_head_rows, d), lambda i, f: (0, 0))
    vec_spec = pl.BlockSpec((1, d), lambda i, f: (0, 0))
    x_specs = [row_spec, tail_spec] if x_tail is not None else [row_spec]
    x_args = [x, x_tail] if x_tail is not None else [x]
    if split_rows is not None:
        out_specs = [row_spec, tail_spec]
        out_shape = [jax.ShapeDtypeStruct((split_rows, d), F32), jax.ShapeDtypeStruct((m - split_rows, d), F32)]
    else:
        out_specs, out_shape = [row_spec], [jax.ShapeDtypeStruct((m, d), F32)]

    cast_in_specs, cast_out_specs, cast_shapes = [], [], []
    for arr, lead in cast:
        rows, cols = arr.shape[-2:]
        slab_rows = next(r for r in range(CAST_ROWS, rows + 1, CAST_ROWS)
                         if rows % r == 0 and rows // r <= n_steps)
        n_slabs = rows // slab_rows
        assert len(lead) == arr.ndim - 2
        slab = lambda i, f, n_slabs=n_slabs: jnp.minimum(i * nf + f, n_slabs - 1)
        cast_in_specs.append(pl.BlockSpec((None,) * len(lead) + (slab_rows, cols),
                                          lambda i, f, lead=lead, slab=slab: (*lead, slab(i, f), 0)))
        cast_out_specs.append(pl.BlockSpec((slab_rows, cols), lambda i, f, slab=slab: (slab(i, f), 0)))
        cast_shapes.append(jax.ShapeDtypeStruct((rows, cols), BF16))

    in_order = split_rows is not None or bool(cast)
    outs = pl.pallas_call(
        functools.partial(_ffn_body, nf=nf, last=n_tiles - 1, cut=cut, pair_in=x_tail is not None,
                          pair_out=split_rows is not None, n_cast=len(cast)),
        grid=(m // tm, nf),
        in_specs=x_specs + [
            vec_spec, vec_spec,
            pl.BlockSpec((d, tf), lambda i, f: (0, f)),
            pl.BlockSpec((d, tf), lambda i, f: (0, f + nf)),
            pl.BlockSpec((tf, d), lambda i, f: (f, 0)),
        ] + cast_in_specs,
        out_specs=out_specs + cast_out_specs,
        out_shape=out_shape + cast_shapes,
        scratch_shapes=[pltpu.VMEM((tm, d), BF16)],
        compiler_params=_params("arbitrary" if in_order else "parallel", "arbitrary"),
        name="ffn",
    )(*x_args, g_pre, g_post, w_in, w_in, w_out, *[arr for arr, _ in cast])
    n_main = len(out_shape)
    main = outs[0] if n_main == 1 else tuple(outs[:n_main])
    return main, list(outs[n_main:])


def _proj_glu_body(x_ref, g_ref, wv_ref, wg_ref, wp_ref, a_ref, p_ref):
    xn = _rms(x_ref[...], g_ref[...]).astype(BF16)
    val = jnp.dot(xn, wv_ref[...], preferred_element_type=F32)
    gate = jnp.dot(xn, wg_ref[...], preferred_element_type=F32)
    a_ref[...] = val * _sigmoid(gate)
    p_ref[...] = jnp.dot(xn, wp_ref[...], preferred_element_type=F32)


def _proj_glu(x, g, w, *, tm=WIDE_ROW_TILE):
    m, d = x.shape
    wspec = lambda part: pl.BlockSpec((d, D_CONV), lambda i: (0, part), pipeline_mode=pl.Buffered(1))
    return pl.pallas_call(
        _proj_glu_body,
        grid=(m // tm,),
        in_specs=[
            pl.BlockSpec((tm, d), lambda i: (i, 0)),
            pl.BlockSpec((1, d), lambda i: (0, 0)),
            wspec(0), wspec(1), wspec(2),
        ],
        out_specs=[
            pl.BlockSpec((tm, D_CONV), lambda i: (i, 0)),
            pl.BlockSpec((tm, D_POOL), lambda i: (i, 0)),
        ],
        out_shape=[jax.ShapeDtypeStruct((m, D_CONV), F32), jax.ShapeDtypeStruct((m, D_POOL), F32)],
        compiler_params=_params("parallel"),
        name="proj_glu",
    )(x, g, w, w, w)


def _proj_hgrn_body(x_ref, g_ref, lb_ref, wq_ref, wf_ref, wv_ref, wg_ref, q_ref, f_ref, v_ref, ga_ref, xn_ref):
    @pl.when(pl.program_id(1) == 0)
    def _():
        xn_ref[...] = _rms(x_ref[...], g_ref[...]).astype(BF16)

    xn = xn_ref[...]
    q = jnp.dot(xn, wq_ref[...], preferred_element_type=F32)
    q_ref[...] = (q * _sigmoid(q) * (DK ** -0.5)).astype(BF16)
    fz = jnp.dot(xn, wf_ref[...], preferred_element_type=F32)
    lb = lb_ref[...]
    f_ref[...] = lb + (1.0 - lb) * _sigmoid(fz)
    v_ref[...] = jnp.dot(xn, wv_ref[...], preferred_element_type=F32).astype(BF16)
    gz = jnp.dot(xn, wg_ref[...], preferred_element_type=F32)
    ga_ref[...] = (gz * _sigmoid(gz)).astype(BF16)


def _proj_hgrn(x, g, lb, w, *, tm=WIDE_ROW_TILE, tn=512):
    m, d = x.shape
    hd = HEADS * DK
    nj = hd // tn
    wspec = lambda part: pl.BlockSpec((d, tn), lambda i, j: (0, j + part * nj))
    ospec = pl.BlockSpec((tm, tn), lambda i, j: (i, j))
    act = jax.ShapeDtypeStruct((m, hd), BF16)
    return pl.pallas_call(
        _proj_hgrn_body,
        grid=(m // tm, nj),
        in_specs=[
            pl.BlockSpec((tm, d), lambda i, j: (i, 0)),
            pl.BlockSpec((1, d), lambda i, j: (0, 0)),
            pl.BlockSpec((1, tn), lambda i, j: (0, j)),
            wspec(0), wspec(1), wspec(2), wspec(3),
        ],
        out_specs=[ospec, ospec, ospec, ospec],
        out_shape=[act, jax.ShapeDtypeStruct((m, hd), F32), act, act],
        scratch_shapes=[pltpu.VMEM((tm, d), BF16)],
        compiler_params=_params("parallel", "arbitrary"),
        name="proj_hgrn",
    )(x, g, lb, w, w, w, w)


def _proj_out_body(yh_ref, yt_ref, w_ref, x_ref, g_ref, o_ref, *, n_head):
    def project(y_ref):
        m = jnp.dot(y_ref[...], w_ref[...], preferred_element_type=F32)
        o_ref[...] = x_ref[...] + _rms(m, g_ref[...])

    _by_trunk(pl.program_id(0), n_head, (yh_ref,), (yt_ref,), project)


def _proj_out(y_head, y_tail, w, x, g, *, tm=ROW_TILE):
    m, d = x.shape
    kdim = w.shape[0]
    n_head = y_head.shape[0] // tm
    return pl.pallas_call(
        functools.partial(_proj_out_body, n_head=n_head),
        grid=(m // tm,),
        in_specs=list(_head_tail_specs((tm, kdim), n_head)) + [
            pl.BlockSpec((kdim, d), lambda i: (0, 0)),
            pl.BlockSpec((tm, d), lambda i: (i, 0)),
            pl.BlockSpec((1, d), lambda i: (0, 0)),
        ],
        out_specs=pl.BlockSpec((tm, d), lambda i: (i, 0)),
        out_shape=jax.ShapeDtypeStruct((m, d), F32),
        compiler_params=_params("parallel"),
        name="proj_out",
    )(y_head, y_tail, w, x, g)


def _convpool_body(a_ref, ah_ref, ahist_ref, p_ref, ph_ref, phist_ref, cw_ref, cb_ref, lng_ref,
                   lnb_ref, pw_ref, ps_ref, o_ref, aext_ref, pext_ref, y_ref, *, t, rc, start_pos):
    i = pl.program_id(1)

    @pl.when(i == 0)
    def _():
        aext_ref[0:CONV_HALO, :] = ahist_ref[0]
        pext_ref[0:POOL_HALO, :] = phist_ref[0]

    @pl.when(i > 0)
    def _():
        aext_ref[0:CONV_HALO, :] = ah_ref[...]
        pext_ref[0:POOL_HALO, :] = ph_ref[...]

    aext_ref[CONV_HALO:CONV_HALO + t, :] = a_ref[...]
    aext_ref[CONV_HALO + t:CONV_HALO + t + SUBLANES, :] = jnp.zeros((SUBLANES, D_CONV), F32)
    pext_ref[POOL_HALO:POOL_HALO + t, :] = p_ref[...]

    lead = CONV_HALO - (CONV_WIDTH - 1)

    def conv_rows(r, carry):
        r0 = pl.multiple_of(r * rc, rc)
        for c in range(D_CONV // 128):
            cs = slice(c * 128, (c + 1) * 128)
            win = aext_ref[pl.ds(r0, rc + CONV_HALO + SUBLANES), cs]
            acc = None
            for ph in range(SUBLANES):
                part = None
                for q in range(CONV_HALO // SUBLANES + 1):
                    w = SUBLANES * q + ph - lead
                    if 0 <= w < CONV_WIDTH:
                        term = win[SUBLANES * q:SUBLANES * q + rc + SUBLANES] * cw_ref[w:w + 1, cs]
                        part = term if part is None else part + term
                shifted = part[ph:ph + rc]
                acc = shifted if acc is None else acc + shifted
            y_ref[pl.ds(r0, rc), cs] = acc + cb_ref[:, cs]
        return carry

    lax.fori_loop(0, t // rc, conv_rows, 0)

    y = y_ref[...]
    mu = jnp.mean(y, axis=-1, keepdims=True)
    yc = y - mu
    z = yc * lax.rsqrt(jnp.mean(yc * yc, axis=-1, keepdims=True) + EPS) * lng_ref[...] + lnb_ref[...]
    o_ref[:, 0:D_CONV] = (z * _sigmoid(z)).astype(BF16)

    pos = start_pos + i * t + lax.broadcasted_iota(jnp.int32, (t, 1), 0)
    for gi, win_len in enumerate(POOL_WINDOWS):
        cs = slice(gi * POOL_GROUP, (gi + 1) * POOL_GROUP)
        cur = pext_ref[POOL_HALO:POOL_HALO + t, cs]
        wsum = cur
        for dlt in range(1, win_len):
            wsum = wsum + pext_ref[POOL_HALO - dlt:POOL_HALO - dlt + t, cs]
        cnt = jnp.minimum(win_len, pos + 1).astype(F32)
        dev = (wsum / cnt - cur).astype(BF16)
        mixed = jnp.dot(dev, pw_ref[gi], preferred_element_type=F32) * ps_ref[:, cs]
        o_ref[:, D_CONV + gi * POOL_GROUP:D_CONV + (gi + 1) * POOL_GROUP] = mixed.astype(BF16)


def _convpool(a, p, a_hist, p_hist, cw, cb, lng, lnb, pw, ps, *, e, batch, seq, row_off, t, start_pos):
    nt = seq // t
    off_t, ca, cp = row_off // t, t // CONV_HALO, t // POOL_HALO
    if t >= CONV_HALO:
        ah_map = lambda b, i: (jnp.maximum((row_off // CONV_HALO) + (b * nt + i) * ca - 1, 0), 0)
        ph_map = lambda b, i: (jnp.maximum((row_off // POOL_HALO) + (b * nt + i) * cp - 1, 0), 0)
    else:
        assert nt == 1
        ah_map = lambda b, i: (0, 0)
        ph_map = lambda b, i: (0, 0)
    vec = lambda n: pl.BlockSpec((1, n), lambda b, i: (0, 0))
    rc = min(t, 64)
    return pl.pallas_call(
        functools.partial(_convpool_body, t=t, rc=rc, start_pos=start_pos),
        grid=(batch, nt),
        in_specs=[
            pl.BlockSpec((t, D_CONV), lambda b, i: (off_t + b * nt + i, 0)),
            pl.BlockSpec((CONV_HALO, D_CONV), ah_map),
            pl.BlockSpec((1, CONV_HALO, D_CONV), lambda b, i: (b, 0, 0)),
            pl.BlockSpec((t, D_POOL), lambda b, i: (off_t + b * nt + i, 0)),
            pl.BlockSpec((POOL_HALO, D_POOL), ph_map),
            pl.BlockSpec((1, POOL_HALO, D_POOL), lambda b, i: (b, 0, 0)),
            pl.BlockSpec((CONV_HALO, D_CONV), lambda b, i: (0, 0)),
            vec(D_CONV), vec(D_CONV), vec(D_CONV),
            pl.BlockSpec((None, len(POOL_WINDOWS), POOL_GROUP, POOL_GROUP), lambda b, i: (e, 0, 0, 0)),
            vec(D_POOL),
        ],
        out_specs=pl.BlockSpec((t, D_CONV + D_POOL), lambda b, i: (b * nt + i, 0)),
        out_shape=jax.ShapeDtypeStruct((batch * seq, D_CONV + D_POOL), BF16),
        scratch_shapes=[
            pltpu.VMEM((CONV_HALO + t + SUBLANES, D_CONV), F32),
            pltpu.VMEM((POOL_HALO + t, D_POOL), F32),
            pltpu.VMEM((t, D_CONV), F32),
        ],
        compiler_params=_params("parallel", "arbitrary"),
        name="convpool",
    )(a, a, a_hist, p, p, p_hist, cw, cb, lng, lnb, pw, ps)


def _hgrn_body(q_ref, f_ref, v_ref, ga_ref, gn_ref, s0_ref, o_ref, sout_ref,
               st_ref, g_ref, m_ref, tril_ref, *, c, cps, nc):
    ci = pl.program_id(1)
    levels = [1 << b for b in range(c.bit_length() - 1)]

    @pl.when(ci == 0)
    def _():
        for h in range(HEADS):
            st_ref[h] = s0_ref[0, h].T
        ri = lax.broadcasted_iota(jnp.int32, (c, c), 0)
        cj = lax.broadcasted_iota(jnp.int32, (c, c), 1)
        tril_ref[...] = (ri >= cj).astype(BF16)
        split = jnp.where(ri > cj, ri ^ cj, 0)
        for li in range(len(levels)):
            m_ref[li] = ((split >> li) == 1).astype(BF16)

    n = c // SUBLANES
    sub = lax.broadcasted_iota(jnp.int32, (1, SUBLANES, DK), 1)
    tiles = lambda arr: arr.reshape(n, SUBLANES, DK)

    def cumulate(j, h):
        hs = slice(h * DK, (h + 1) * DK)
        rs = slice(j * c, (j + 1) * c)
        lf = jnp.maximum(jnp.log(f_ref[rs, hs]) * LOG2_E, LOG2_FLOOR)
        hi = lf.astype(BF16)
        rest = lf - hi.astype(F32)
        mid = rest.astype(BF16)
        lo = (rest - mid.astype(F32)).astype(BF16)
        sums = jnp.dot(tril_ref[...], jnp.concatenate([hi, mid, lo], axis=1), preferred_element_type=F32)
        g_ref[rs, hs] = sums[:, 0:DK] + sums[:, DK:2 * DK] + sums[:, 2 * DK:3 * DK]

    def attend(j, h):
        hs = slice(h * DK, (h + 1) * DK)
        rs = slice(j * c, (j + 1) * c)
        q = q_ref[rs, hs].astype(F32)
        f = f_ref[rs, hs]
        k = 1.0 - f
        vb = v_ref[rs, hs]
        v = vb.astype(F32)
        g = g_ref[rs, hs]
        q3, k3, f3, g3 = tiles(q), tiles(k), tiles(f), tiles(g)
        qf3 = q3 * f3

        xs = []
        for s in levels:
            if s == 1:
                x = jnp.where((sub & 1) == 1, qf3, k3)
            elif s == 2:
                f_prev = pltpu.roll(f3, 1, 1)
                f_next = pltpu.roll(f3, SUBLANES - 1, 1)
                r4 = sub & 3
                x = jnp.where(r4 == 0, k3 * f_next,
                              jnp.where(r4 == 1, k3, jnp.where(r4 == 2, qf3, qf3 * f_prev)))
            elif s == 4:
                ref = jnp.broadcast_to(g3[:, 3:4, :], g3.shape)
                upper = (sub & 4) != 0
                x = jnp.where(upper, q3, k3) * jnp.exp2(jnp.where(upper, g3 - ref, ref - g3))
            else:
                pieces = []
                for pair in range(c // (2 * s)):
                    b0 = pair * 2 * s
                    ref = g_ref[j * c + b0 + s - 1:j * c + b0 + s, hs]
                    pieces.append(k[b0:b0 + s] * jnp.exp2(ref - g[b0:b0 + s]))
                    pieces.append(q[b0 + s:b0 + 2 * s] * jnp.exp2(g[b0 + s:b0 + 2 * s] - ref))
                x = jnp.concatenate(pieces, axis=0)
            xs.append(x.reshape(c, DK).astype(BF16))

        att = None
        for li, xb in enumerate(xs):
            term = lax.dot_general(xb, xb, NT_DIMS, preferred_element_type=F32).astype(BF16) * m_ref[li]
            att = term if att is None else att + term
        return q, k, v, vb, g, att

    def finish(j, h, q, k, v, vb, g, att):
        hs = slice(h * DK, (h + 1) * DK)
        rs = slice(j * c, (j + 1) * c)
        o = jnp.dot(att, vb, preferred_element_type=F32)
        o = o + jnp.sum(q * k, axis=-1, keepdims=True) * v

        st = st_ref[h]
        qd = (q * jnp.exp2(g)).astype(BF16)
        o = o + lax.dot_general(qd, st.astype(BF16), NT_DIMS, preferred_element_type=F32)

        g_last = g_ref[j * c + c - 1:j * c + c, hs]
        kd = (k * jnp.exp2(g_last - g)).astype(BF16)
        st_ref[h] = st * jnp.exp2(g_last) + lax.dot_general(vb, kd, TN_DIMS, preferred_element_type=F32)

        on = o * lax.rsqrt(jnp.mean(o * o, axis=-1, keepdims=True) + EPS) * gn_ref[...]
        o_ref[rs, hs] = (on * ga_ref[rs, hs].astype(F32)).astype(BF16)

    tasks = [(j, h) for j in range(cps) for h in range(HEADS)]
    pending = {}
    for step in range(len(tasks) + 2 * HEAD_SKEW):
        if step < len(tasks):
            cumulate(*tasks[step])
        if 0 <= step - HEAD_SKEW < len(tasks):
            pending[step - HEAD_SKEW] = attend(*tasks[step - HEAD_SKEW])
        if 0 <= step - 2 * HEAD_SKEW < len(tasks):
            finish(*tasks[step - 2 * HEAD_SKEW], *pending.pop(step - 2 * HEAD_SKEW))

    @pl.when(ci == nc - 1)
    def _():
        for h in range(HEADS):
            sout_ref[0, h] = st_ref[h].T


def _hgrn(q, f, v, ga, gnorm, s0, *, batch, seq, row_off, c, cps=1):
    blk = c * cps
    nc = seq // blk
    off = row_off // blk
    hd = HEADS * DK
    rows = pl.BlockSpec((blk, hd), lambda b, i: (off + b * nc + i, 0))
    state = pl.BlockSpec((1, HEADS, DK, DV), lambda b, i: (b, 0, 0, 0))
    n_levels = c.bit_length() - 1
    return pl.pallas_call(
        functools.partial(_hgrn_body, c=c, cps=cps, nc=nc),
        grid=(batch, nc),
        in_specs=[rows, rows, rows, rows, pl.BlockSpec((1, DV), lambda b, i: (0, 0)), state],
        out_specs=[pl.BlockSpec((blk, hd), lambda b, i: (b * nc + i, 0)), state],
        out_shape=[
            jax.ShapeDtypeStruct((batch * seq, hd), BF16),
            jax.ShapeDtypeStruct((batch, HEADS, DK, DV), F32),
        ],
        scratch_shapes=[
            pltpu.VMEM((HEADS, DV, DK), F32),
            pltpu.VMEM((blk, hd), F32),
            pltpu.VMEM((n_levels, c, c), BF16),
            pltpu.VMEM((c, c), BF16),
        ],
        compiler_params=_params("parallel", "arbitrary"),
        name="hgrn",
    )(q, f, v, ga, gnorm, s0)


def _pad_hist(h, rows):
    b, n, d = h.shape
    return jnp.concatenate([jnp.zeros((b, rows - n, d), h.dtype), h], axis=1)


def _new_cache(hist, rows, batch, seq, off, keep):
    if seq >= keep:
        return jnp.stack([rows[off + (b + 1) * seq - keep:off + (b + 1) * seq] for b in range(batch)])
    cur = rows[off:off + batch * seq].reshape(batch, seq, rows.shape[1])
    return jnp.concatenate([hist[:, seq:], cur], axis=1)


def kernel(x_prompt, x_sample, cache_conv, cache_pool, state_hgrn, ab_w_in, ab_w_out, conv_w, conv_b,
           conv_ln_g, conv_ln_b, pool_w, pool_scale, hgrn_w_in, hgrn_w_out, hgrn_gnorm, hgrn_lb, ffn_w_in,
           ffn_w_out, norm_g):
    bp, lp, d = x_prompt.shape
    bs, ls, _ = x_sample.shape
    mp, ms = bp * lp, bs * ls
    depth = ffn_w_in.shape[0]
    row = lambda vct: vct.reshape(1, -1).astype(F32)

    lb_all = jnp.cumsum(jax.nn.softmax(hgrn_lb.astype(F32), axis=0), axis=0)
    lb_all = lb_all - lb_all[0:1]

    pool_w_b = pool_w.astype(BF16)
    w_in_b, w_out_b = ffn_w_in[0, 0].astype(BF16), ffn_w_out[0, 0].astype(BF16)

    conv_p, pool_p, hgrn_p, conv_s, pool_s, hgrn_s = [], [], [], [], [], []
    x = None
    for l in range(depth):
        g = norm_g[l]
        mixer_w = (ab_w_in, ab_w_out) if l % 2 == 0 else (hgrn_w_in, hgrn_w_out)
        cast = [(ffn_w_in, (l, 1)), (ffn_w_out, (l, 1)), (mixer_w[0], (l // 2,)), (mixer_w[1], (l // 2,))]
        if l == 0:
            x, cast_w = _ffn(x_prompt.reshape(mp, d), row(g[0]), row(0.5 * g[1]), w_in_b, w_out_b,
                             x_tail=x_sample.reshape(ms, d), cast=cast)
        else:
            x, cast_w = _ffn(x, row(g[0]), row(0.5 * g[1]), w_in_b, w_out_b, cast=cast)
        w_in_b, w_out_b, mix_in_b, mix_out_b = cast_w
        if l % 2 == 0:
            e = l // 2
            a, p = _proj_glu(x, row(g[2]), mix_in_b)
            cw = jnp.concatenate([conv_w[e], jnp.zeros((CONV_HALO - CONV_WIDTH, D_CONV), F32)], axis=0)
            shared = (cw, row(conv_b[e]), row(conv_ln_g[e]), row(conv_ln_b[e]), pool_w_b, row(pool_scale[e]))
            hist_c, hist_p = cache_conv[e].astype(F32), cache_pool[e].astype(F32)
            zero_c = jnp.zeros((bp, CONV_WIDTH - 1, D_CONV), F32)
            zero_p = jnp.zeros((bp, POOL_MAX - 1, D_POOL), F32)
            y_p = _convpool(a, p, _pad_hist(zero_c, CONV_HALO), _pad_hist(zero_p, POOL_HALO), *shared,
                            e=e, batch=bp, seq=lp, row_off=0, t=512, start_pos=0)
            y_s = _convpool(a, p, _pad_hist(hist_c, CONV_HALO), _pad_hist(hist_p, POOL_HALO), *shared,
                            e=e, batch=bs, seq=ls, row_off=mp, t=ls, start_pos=PAST_LEN)
            conv_p.append(_new_cache(zero_c, a, bp, lp, 0, CONV_WIDTH - 1))
            pool_p.append(_new_cache(zero_p, p, bp, lp, 0, POOL_MAX - 1))
            conv_s.append(_new_cache(hist_c, a, bs, ls, mp, CONV_WIDTH - 1))
            pool_s.append(_new_cache(hist_p, p, bs, ls, mp, POOL_MAX - 1))
        else:
            o = l // 2
            q, f, v, ga = _proj_hgrn(x, row(g[2]), row(lb_all[l]), mix_in_b)
            gn = row(hgrn_gnorm[o])
            zero_s = jnp.zeros((bp, HEADS, DK, DV), F32)
            y_p, s_p = _hgrn(q, f, v, ga, gn, zero_s, batch=bp, seq=lp, row_off=0, c=128, cps=2)
            y_s, s_s = _hgrn(q, f, v, ga, gn, state_hgrn[o].astype(F32), batch=bs, seq=ls, row_off=mp, c=ls)
            hgrn_p.append(s_p)
            hgrn_s.append(s_s.astype(state_hgrn.dtype))
        x = _proj_out(y_p, y_s, mix_out_b, x, row(g[3]))
        if l == depth - 1:
            (x_p, x_s), _ = _ffn(x, row(g[4]), row(0.5 * g[5]), w_in_b, w_out_b, split_rows=mp)
        else:
            x, (w_in_b, w_out_b) = _ffn(x, row(g[4]), row(0.5 * g[5]), w_in_b, w_out_b,
                                        cast=[(ffn_w_in, (l + 1, 0)), (ffn_w_out, (l + 1, 0))])

    return (x_p.reshape(bp, lp, d), x_s.reshape(bs, ls, d),
            jnp.stack(conv_p), jnp.stack(pool_p), jnp.stack(hgrn_p),
            jnp.stack(conv_s), jnp.stack(pool_s), jnp.stack(hgrn_s))
```
